```python
import jax, jax.numpy as jnp
from jax import lax
import numpy as np


D_MODEL = 2048
BATCH = 16
SEQ = 2048
DEPTH = 2
DEC_BATCH = 4
DEC_SEQ = 4096
PAST_LEN = 128

N_META = 16
D_MIX = D_MODEL
D_CONV = D_MIX // 4
D_LRU = D_MIX // 2
D_POOL = D_MIX // 4
N_LRU_HEADS = 8
LRU_HEAD_DIM = D_LRU // N_LRU_HEADS
POOL_WINDOWS = (2, 4, 8, 16)
N_POOL_GROUPS = len(POOL_WINDOWS)
POOL_GROUP_DIM = D_POOL // N_POOL_GROUPS
SHORT_CONV_WIDTH = 3
LRU_CONV_WIDTH = 4
LRU_C = 8.0
D_FF = 5632
D_IN_PROJ = 3 * D_CONV + 2 * D_LRU + D_POOL
IN_PROJ_SPLITS = (D_CONV, 2 * D_CONV, 3 * D_CONV, 3 * D_CONV + D_LRU, 3 * D_CONV + 2 * D_LRU)
DEEPNORM_ALPHA = (2 * DEPTH) ** 0.25
DEEPNORM_BETA = (8 * DEPTH) ** -0.25
LN_EPS = 1e-5

kernel_name = 'hymba_parallel_conv_rglru_pool_encoder'


def _layernorm(x, g, b):
    xf = x.astype(jnp.float32)
    mu = jnp.mean(xf, axis=-1, keepdims=True)
    var = jnp.mean(jnp.square(xf - mu), axis=-1, keepdims=True)
    y = (xf - mu) * lax.rsqrt(var + LN_EPS) * g.astype(jnp.float32) + b.astype(jnp.float32)
    return y.astype(x.dtype)


def _post_norm(x, sub, g, b):
    return _layernorm(DEEPNORM_ALPHA * x + sub, g, b)


def _swiglu(x, w_gate, w_up, w_down):
    return (jax.nn.silu(x @ w_gate) * (x @ w_up)) @ w_down


def _short_conv_mixer(b_gate, c_gate, v, conv_w, conv_b):
    u = c_gate * v
    L = u.shape[1]
    pad = SHORT_CONV_WIDTH // 2
    up = jnp.pad(u, ((0, 0), (pad, pad), (0, 0)))
    y = conv_b + sum(up[:, k:k + L] * conv_w[k] for k in range(SHORT_CONV_WIDTH))
    return b_gate * y


def _causal_conv(u, w, b):
    K = w.shape[0]
    L = u.shape[1]
    up = jnp.pad(u, ((0, 0), (K - 1, 0), (0, 0)))
    return b + sum(up[:, k:k + L] * w[k] for k in range(K))


def _linear_combine(left, right):
    a1, b1 = left
    a2, b2 = right
    return a1 * a2, a2 * b1 + b2


def _rglru_direction(u, conv_w, conv_b, w_a, b_a, w_x, b_x, lam):
    xc = _causal_conv(u, conv_w, conv_b)
    Bsz, L, _ = xc.shape
    xh = xc.reshape(Bsz, L, N_LRU_HEADS, LRU_HEAD_DIM)
    r = jax.nn.sigmoid(jnp.einsum('blhi,hij->blhj', xh, w_a).reshape(Bsz, L, D_LRU) + b_a)
    i = jax.nn.sigmoid(jnp.einsum('blhi,hij->blhj', xh, w_x).reshape(Bsz, L, D_LRU) + b_x)
    log_a = (-LRU_C * jax.nn.softplus(-lam.astype(jnp.float32))) * r.astype(jnp.float32)
    a = jnp.exp(log_a)
    mult = jnp.sqrt(-jnp.expm1(2.0 * log_a))
    bterm = mult * (i * xc).astype(jnp.float32)
    _, h = lax.associative_scan(_linear_combine, (a, bterm), axis=1)
    return h.astype(u.dtype)


def _bidirectional_rglru(lru_x, lru_gate, conv_w, conv_b, w_a, b_a, w_x, b_x, lam):
    h_f = _rglru_direction(lru_x, conv_w[0], conv_b[0], w_a[0], b_a[0], w_x[0], b_x[0], lam[0])
    h_b = jnp.flip(_rglru_direction(jnp.flip(lru_x, axis=1), conv_w[1], conv_b[1], w_a[1], b_a[1],
                                    w_x[1], b_x[1], lam[1]), axis=1)
    return jax.nn.gelu(lru_gate) * (h_f + h_b)


def _pool_mixer(u, pool_w, pool_scale):
    Bsz, L, _ = u.shape
    uf = u.astype(jnp.float32)
    cs = jnp.concatenate([jnp.zeros((Bsz, 1, D_POOL), jnp.float32), jnp.cumsum(uf, axis=1)], axis=1)
    t = jnp.arange(L)
    outs = []
    for g, w in enumerate(POOL_WINDOWS):
        lo = jnp.maximum(t - w // 2, 0)
        hi = jnp.minimum(t + w // 2 - 1, L - 1)
        sl = slice(g * POOL_GROUP_DIM, (g + 1) * POOL_GROUP_DIM)
        csg = cs[..., sl]
        win_sum = jnp.take(csg, hi + 1, axis=1) - jnp.take(csg, lo, axis=1)
        count = (hi - lo + 1).astype(jnp.float32)[None, :, None]
        outs.append(win_sum / count - uf[..., sl])
    pooled = jnp.stack(outs, axis=2)
    mixed = jnp.einsum('blgi,gij->blgj', pooled, pool_w.astype(jnp.float32)).reshape(Bsz, L, D_POOL)
    return (mixed * pool_scale.astype(jnp.float32)).astype(u.dtype)


def _trunk(x, meta_tokens, ln_in_g, ln_in_b, ffn1_w_gate, ffn1_w_up, ffn1_w_down, ln1_g, ln1_b,
           w_in, conv_w, conv_b, lru_conv_w, lru_conv_b, lru_w_a, lru_b_a, lru_w_x, lru_b_x, lru_lambda,
           pool_w, pool_scale, w_out, ln2_g, ln2_b, ffn2_w_gate, ffn2_w_up, ffn2_w_down, ln3_g, ln3_b):
    Bsz = x.shape[0]
    meta = jnp.broadcast_to(meta_tokens.astype(x.dtype)[None], (Bsz, N_META, D_MODEL))
    h = _layernorm(jnp.concatenate([meta, x], axis=1), ln_in_g, ln_in_b)
    for l in range(DEPTH):
        h = _post_norm(h, 0.5 * _swiglu(h, ffn1_w_gate[l], ffn1_w_up[l], ffn1_w_down[l]), ln1_g[l], ln1_b[l])
        proj = h @ w_in[l]
        b_gate, c_gate, v, lru_x, lru_gate, pool_in = jnp.split(proj, IN_PROJ_SPLITS, axis=-1)
        y_conv = _short_conv_mixer(b_gate, c_gate, v, conv_w[l], conv_b[l])
        y_lru = _bidirectional_rglru(lru_x, lru_gate, lru_conv_w[l], lru_conv_b[l], lru_w_a[l], lru_b_a[l],
                                     lru_w_x[l], lru_b_x[l], lru_lambda[l])
        y_pool = _pool_mixer(pool_in, pool_w[l], pool_scale[l])
        mix = jnp.concatenate([y_conv, y_lru, y_pool], axis=-1) @ w_out[l]
        h = _post_norm(h, mix, ln2_g[l], ln2_b[l])
        h = _post_norm(h, 0.5 * _swiglu(h, ffn2_w_gate[l], ffn2_w_up[l], ffn2_w_down[l]), ln3_g[l], ln3_b[l])
    return h[:, N_META:]


def setup_inputs(seed: int = 0) -> dict:
    key = jax.random.key(seed)
    ks = jax.random.split(key, 32)
    f32 = jnp.float32

    def nrm(k, shape, scale):
        return jax.random.normal(k, shape, f32) * scale

    def gain(k, shape):
        return 1.0 + 0.02 * jax.random.normal(k, shape, f32)

    u = jax.random.uniform(ks[17], (DEPTH, 2, D_LRU), f32, minval=0.9, maxval=0.999)
    a_base = u ** (1.0 / LRU_C)
    lru_lambda = jnp.log(a_base) - jnp.log1p(-a_base)

    return {
        'x_prompt': nrm(ks[0], (BATCH, SEQ, D_MODEL), 1.0),
        'x_sample': nrm(ks[1], (DEC_BATCH, DEC_SEQ, D_MODEL), 1.0),
        'meta_tokens': nrm(ks[2], (N_META, D_MODEL), 1.0),
        'ln_in_g': gain(ks[3], (D_MODEL,)),
        'ln_in_b': nrm(ks[4], (D_MODEL,), 0.02),
        'ffn1_w_gate': nrm(ks[5], (DEPTH, D_MODEL, D_FF), D_MODEL ** -0.5),
        'ffn1_w_up': nrm(ks[6], (DEPTH, D_MODEL, D_FF), D_MODEL ** -0.5),
        'ffn1_w_down': nrm(ks[7], (DEPTH, D_FF, D_MODEL), DEEPNORM_BETA * D_FF ** -0.5),
        'ln1_g': gain(ks[8], (DEPTH, D_MODEL)),
        'ln1_b': nrm(ks[9], (DEPTH, D_MODEL), 0.02),
        'w_in': nrm(ks[10], (DEPTH, D_MODEL, D_IN_PROJ), D_MODEL ** -0.5),
        'conv_w': nrm(ks[11], (DEPTH, SHORT_CONV_WIDTH, D_CONV), SHORT_CONV_WIDTH ** -0.5),
        'conv_b': nrm(ks[12], (DEPTH, D_CONV), 0.02),
        'lru_conv_w': nrm(ks[13], (DEPTH, 2, LRU_CONV_WIDTH, D_LRU), LRU_CONV_WIDTH ** -0.5),
        'lru_conv_b': nrm(ks[14], (DEPTH, 2, D_LRU), 0.02),
        'lru_w_a': nrm(ks[15], (DEPTH, 2, N_LRU_HEADS, LRU_HEAD_DIM, LRU_HEAD_DIM), LRU_HEAD_DIM ** -0.5),
        'lru_b_a': nrm(ks[16], (DEPTH, 2, D_LRU), 0.02),
        'lru_w_x': nrm(ks[18], (DEPTH, 2, N_LRU_HEADS, LRU_HEAD_DIM, LRU_HEAD_DIM), LRU_HEAD_DIM ** -0.5),
        'lru_b_x': nrm(ks[19], (DEPTH, 2, D_LRU), 0.02),
        'lru_lambda': lru_lambda,
        'pool_w': nrm(ks[20], (DEPTH, N_POOL_GROUPS, POOL_GROUP_DIM, POOL_GROUP_DIM), POOL_GROUP_DIM ** -0.5),
        'pool_scale': gain(ks[21], (DEPTH, D_POOL)),
        'w_out': nrm(ks[22], (DEPTH, D_MIX, D_MODEL), DEEPNORM_BETA * D_MIX ** -0.5),
        'ln2_g': gain(ks[23], (DEPTH, D_MODEL)),
        'ln2_b': nrm(ks[24], (DEPTH, D_MODEL), 0.02),
        'ffn2_w_gate': nrm(ks[25], (DEPTH, D_MODEL, D_FF), D_MODEL ** -0.5),
        'ffn2_w_up': nrm(ks[26], (DEPTH, D_MODEL, D_FF), D_MODEL ** -0.5),
        'ffn2_w_down': nrm(ks[27], (DEPTH, D_FF, D_MODEL), DEEPNORM_BETA * D_FF ** -0.5),
        'ln3_g': gain(ks[28], (DEPTH, D_MODEL)),
        'ln3_b': nrm(ks[29], (DEPTH, D_MODEL), 0.02),
    }


def reference(x_prompt, x_sample, meta_tokens, ln_in_g, ln_in_b, ffn1_w_gate, ffn1_w_up, ffn1_w_down,
              ln1_g, ln1_b, w_in, conv_w, conv_b, lru_conv_w, lru_conv_b, lru_w_a, lru_b_a, lru_w_x, lru_b_x,
              lru_lambda, pool_w, pool_scale, w_out, ln2_g, ln2_b, ffn2_w_gate, ffn2_w_up, ffn2_w_down,
              ln3_g, ln3_b):
    weights = (meta_tokens, ln_in_g, ln_in_b, ffn1_w_gate, ffn1_w_up, ffn1_w_down, ln1_g, ln1_b,
               w_in, conv_w, conv_b, lru_conv_w, lru_conv_b, lru_w_a, lru_b_a, lru_w_x, lru_b_x, lru_lambda,
               pool_w, pool_scale, w_out, ln2_g, ln2_b, ffn2_w_gate, ffn2_w_up, ffn2_w_down, ln3_g, ln3_b)
    y_prompt = _trunk(x_prompt, *weights)
    y_sample = _trunk(x_sample, *weights)
    return (y_prompt, y_sample)
```

```python
import functools

import jax
import jax.numpy as jnp
from jax import lax
from jax.experimental import pallas as pl
from jax.experimental.pallas import tpu as pltpu

F32 = jnp.float32
BF16 = jnp.bfloat16

N_META = 16
POOL_WINDOWS = (2, 4, 8, 16)
SHORT_CONV_WIDTH = 3
LRU_CONV_WIDTH = 4
LRU_C = 8.0
LN_EPS = 1e-5

LANES = 128
SUBLANES = 8
BF16_SUBLANES = 16
VMEM_BYTES_V7X = 64 * 1024 * 1024

ROW_TILE = 512
FF_TILE = 512
PROJ_TILE = 1024
MIX_CHUNK = 128
COPY_CHUNK = 256
MARGIN = SUBLANES


def _vmem_limit(block_bytes):
    return int(min(block_bytes + block_bytes // 2 + (8 << 20), VMEM_BYTES_V7X - (6 << 20)))


def _layernorm(z, g, b):
    mu = jnp.mean(z, axis=-1, keepdims=True)
    d = z - mu
    var = jnp.mean(d * d, axis=-1, keepdims=True)
    return d * lax.rsqrt(var + LN_EPS) * g + b


def _ln_kernel(x_ref, g_ref, b_ref, o_ref):
    o_ref[...] = _layernorm(x_ref[...], g_ref[...], b_ref[...])


def _ln_call(x, g, b):
    T, D = x.shape
    return pl.pallas_call(
        _ln_kernel,
        grid=(T // ROW_TILE,),
        in_specs=[pl.BlockSpec((ROW_TILE, D), lambda i: (i, 0)),
                  pl.BlockSpec((1, D), lambda i: (0, 0)),
                  pl.BlockSpec((1, D), lambda i: (0, 0))],
        out_specs=pl.BlockSpec((ROW_TILE, D), lambda i: (i, 0)),
        out_shape=jax.ShapeDtypeStruct((T, D), F32),
        compiler_params=pltpu.CompilerParams(
            dimension_semantics=("parallel",),
            vmem_limit_bytes=_vmem_limit(4 * ROW_TILE * D * 4)),
        name="ln_in",
    )(x, g.reshape(1, D), b.reshape(1, D))


def _ffn_kernel(x_ref, wg_ref, wu_ref, wd_ref, g_ref, b_ref, o_ref, xb_ref, acc_ref, *, nk, alpha):
    k = pl.program_id(1)

    @pl.when(k == 0)
    def _():
        xb_ref[...] = x_ref[...].astype(BF16)
        acc_ref[...] = jnp.zeros_like(acc_ref)

    xb = xb_ref[...]
    gate = jnp.dot(xb, wg_ref[...], preferred_element_type=F32)
    up = jnp.dot(xb, wu_ref[...], preferred_element_type=F32)
    hidden = (gate * jax.nn.sigmoid(gate) * up).astype(BF16)
    acc_ref[...] += jnp.dot(hidden, wd_ref[...], preferred_element_type=F32)

    @pl.when(k == nk - 1)
    def _():
        z = alpha * x_ref[...] + 0.5 * acc_ref[...]
        o_ref[...] = _layernorm(z, g_ref[...], b_ref[...])


def _ffn_call(h, wg, wu, wd, g, b, layer, alpha):
    T, D = h.shape
    F = wg.shape[-1]
    nk = F // FF_TILE
    blocks = (4 * ROW_TILE * D * 4 + 6 * D * FF_TILE * 2 + ROW_TILE * D * (2 + 4)
              + 4 * ROW_TILE * FF_TILE * 4)
    return pl.pallas_call(
        functools.partial(_ffn_kernel, nk=nk, alpha=alpha),
        grid=(T // ROW_TILE, nk),
        in_specs=[pl.BlockSpec((ROW_TILE, D), lambda i, k: (i, 0)),
                  pl.BlockSpec((None, D, FF_TILE), lambda i, k: (layer, 0, k)),
                  pl.BlockSpec((None, D, FF_TILE), lambda i, k: (layer, 0, k)),
                  pl.BlockSpec((None, FF_TILE, D), lambda i, k: (layer, k, 0)),
                  pl.BlockSpec((None, 1, D), lambda i, k: (layer, 0, 0)),
                  pl.BlockSpec((None, 1, D), lambda i, k: (layer, 0, 0))],
        out_specs=pl.BlockSpec((ROW_TILE, D), lambda i, k: (i, 0)),
        out_shape=jax.ShapeDtypeStruct((T, D), F32),
        scratch_shapes=[pltpu.VMEM((ROW_TILE, D), BF16), pltpu.VMEM((ROW_TILE, D), F32)],
        compiler_params=pltpu.CompilerParams(
            dimension_semantics=("parallel", "arbitrary"),
            vmem_limit_bytes=_vmem_limit(blocks)),
        name="ffn",
    )(h, wg, wu, wd, g, b)


def _inproj_kernel(x_ref, w_ref, o_ref, xb_ref):
    @pl.when(pl.program_id(1) == 0)
    def _():
        xb_ref[...] = x_ref[...].astype(BF16)

    o_ref[...] = jnp.dot(xb_ref[...], w_ref[...], preferred_element_type=F32)


def _inproj_call(h, w_in, layer):
    T, D = h.shape
    P = w_in.shape[-1]
    blocks = 2 * ROW_TILE * D * 4 + 2 * D * PROJ_TILE * 2 + 2 * ROW_TILE * PROJ_TILE * 4 + ROW_TILE * D * 2
    return pl.pallas_call(
        _inproj_kernel,
        grid=(T // ROW_TILE, P // PROJ_TILE),
        in_specs=[pl.BlockSpec((ROW_TILE, D), lambda i, j: (i, 0)),
                  pl.BlockSpec((None, D, PROJ_TILE), lambda i, j: (layer, 0, j))],
        out_specs=pl.BlockSpec((ROW_TILE, PROJ_TILE), lambda i, j: (i, j)),
        out_shape=jax.ShapeDtypeStruct((T, P), F32),
        scratch_shapes=[pltpu.VMEM((ROW_TILE, D), BF16)],
        compiler_params=pltpu.CompilerParams(
            dimension_semantics=("parallel", "arbitrary"),
            vmem_limit_bytes=_vmem_limit(blocks)),
        name="in_proj",
    )(h, w_in)


def _outproj_kernel(h_ref, y_ref, w_ref, g_ref, b_ref, o_ref, *, alpha):
    mix = jnp.dot(y_ref[...], w_ref[...], preferred_element_type=F32)
    o_ref[...] = _layernorm(alpha * h_ref[...] + mix, g_ref[...], b_ref[...])


def _outproj_call(h, y, w_out, g, b, layer, alpha):
    T, D = h.shape
    blocks = 4 * ROW_TILE * D * 4 + 2 * ROW_TILE * D * 2 + 2 * D * D * 2 + 2 * ROW_TILE * D * 4
    return pl.pallas_call(
        functools.partial(_outproj_kernel, alpha=alpha),
        grid=(T // ROW_TILE,),
        in_specs=[pl.BlockSpec((ROW_TILE, D), lambda i: (i, 0)),
                  pl.BlockSpec((ROW_TILE, D), lambda i: (i, 0)),
                  pl.BlockSpec((None, D, D), lambda i: (layer, 0, 0)),
                  pl.BlockSpec((None, 1, D), lambda i: (layer, 0, 0)),
                  pl.BlockSpec((None, 1, D), lambda i: (layer, 0, 0))],
        out_specs=pl.BlockSpec((ROW_TILE, D), lambda i: (i, 0)),
        out_shape=jax.ShapeDtypeStruct((T, D), F32),
        compiler_params=pltpu.CompilerParams(
            dimension_semantics=("parallel",),
            vmem_limit_bytes=_vmem_limit(blocks)),
        name="out_proj",
    )(h, y, w_out, g, b)


def _fill_padded(dst_ref, L, read):
    lx = L - N_META
    zeros = jnp.zeros((MARGIN, LANES), F32)
    dst_ref[0:MARGIN] = zeros
    dst_ref[MARGIN + L:2 * MARGIN + L] = zeros
    dst_ref[MARGIN:MARGIN + N_META] = read(pl.ds(lx, N_META))

    def body(i, c):
        r = pl.multiple_of(i * COPY_CHUNK, COPY_CHUNK)
        dst = pl.multiple_of(r + (MARGIN + N_META), SUBLANES)
        dst_ref[pl.ds(dst, COPY_CHUNK)] = read(pl.ds(r, COPY_CHUNK))
        return c

    lax.fori_loop(0, lx // COPY_CHUNK, body, 0)


def _for_storage_chunks(L, fn):
    lx = L - N_META

    def body(i, c):
        r = pl.multiple_of(i * COPY_CHUNK, COPY_CHUNK)
        fn(pl.ds(r, COPY_CHUNK), r + N_META, COPY_CHUNK)
        return c

    lax.fori_loop(0, lx // COPY_CHUNK, body, 0)
    fn(pl.ds(lx, N_META), 0, N_META)


def _conv_kernel(bg_ref, cg_ref, v_ref, w_ref, cb_ref, yprev_ref, o_ref, upad_ref, *, L):
    del yprev_ref
    _fill_padded(upad_ref, L, lambda rows: cg_ref[rows] * v_ref[rows])
    w = w_ref[...]
    cb = cb_ref[...]

    def emit(rows, t0, n):
        acc = cb
        for k in range(SHORT_CONV_WIDTH):
            off = k - SHORT_CONV_WIDTH // 2
            acc = acc + w[k:k + 1] * upad_ref[pl.ds(t0 + (MARGIN + off), n)]
        o_ref[rows] = (bg_ref[rows] * acc).astype(BF16)

    _for_storage_chunks(L, emit)


def _pool_kernel(u_ref, pw_ref, ps_ref, yprev_ref, o_ref, upad_ref, *, L):
    del yprev_ref
    _fill_padded(upad_ref, L, lambda rows: u_ref[rows])
    pw = pw_ref[...]
    ps = ps_ref[...]
    group = pl.program_id(1)

    for gi, win in enumerate(POOL_WINDOWS):
        half = win // 2

        @pl.when(group == gi)
        def _(win=win, half=half):
            def emit(rows, t0, n):
                total = upad_ref[pl.ds(t0 + (MARGIN - half), n)]
                for j in range(1 - half, half):
                    total = total + upad_ref[pl.ds(t0 + (MARGIN + j), n)]
                t = t0 + lax.broadcasted_iota(jnp.int32, (n, LANES), 0)
                lo = jnp.maximum(t - half, 0)
                hi = jnp.minimum(t + (half - 1), L - 1)
                count = (hi - lo + 1).astype(F32)
                pooled = total / count - upad_ref[pl.ds(t0 + MARGIN, n)]
                mixed = jnp.dot(pooled.astype(BF16), pw, preferred_element_type=F32)
                o_ref[rows] = (mixed * ps).astype(BF16)

            _for_storage_chunks(L, emit)


def _lru_kernel(x_ref, gate_ref, cw_ref, cb_ref, wg_ref, bg_ref, lam_ref, yprev_ref, o_ref,
                xpad_ref, af_ref, bf_ref, ab_ref, bb_ref, carry_ref, *, L):
    del yprev_ref
    seg = L // SUBLANES
    _fill_padded(xpad_ref, L, lambda rows: x_ref[rows])
    a_refs = (af_ref, ab_ref)
    b_refs = (bf_ref, bb_ref)

    cw = cw_ref[...]
    cb = cb_ref[...]
    bg = bg_ref[...]
    neg_lam = -lam_ref[...]
    softplus = jnp.maximum(neg_lam, 0.0) + jnp.log1p(jnp.exp(-jnp.abs(neg_lam)))
    cvec = -LRU_C * softplus

    def gates(t0, n):
        for d in range(2):
            xc = cb[d:d + 1]
            for k in range(LRU_CONV_WIDTH):
                off = (k - (LRU_CONV_WIDTH - 1)) if d == 0 else ((LRU_CONV_WIDTH - 1) - k)
                xc = xc + cw[d, k:k + 1] * xpad_ref[pl.ds(t0 + (MARGIN + off), n)]
            pre = jnp.dot(xc.astype(BF16), wg_ref[d], preferred_element_type=F32) + bg[d:d + 1]
            r = jax.nn.sigmoid(pre[:, :LANES])
            i = jax.nn.sigmoid(pre[:, LANES:])
            log_a = cvec[d:d + 1] * r
            a = jnp.exp(log_a)
            a_refs[d][pl.ds(t0, n)] = a
            b_refs[d][pl.ds(t0, n)] = jnp.sqrt(1.0 - a * a) * (i * xc)

    n_main = L // MIX_CHUNK

    def gates_body(i, c):
        gates(pl.multiple_of(i * MIX_CHUNK, MIX_CHUNK), MIX_CHUNK)
        return c

    lax.fori_loop(0, n_main, gates_body, 0)
    if L % MIX_CHUNK:
        gates(n_main * MIX_CHUNK, L % MIX_CHUNK)

    def scan_body(t, c):
        hf, pf, hb, pb = c
        rows = pl.ds(t, SUBLANES, stride=seg)
        a = af_ref[rows]
        hf = a * hf + bf_ref[rows]
        pf = pf * a
        bf_ref[rows] = hf
        af_ref[rows] = pf
        rows = pl.ds(seg - 1 - t, SUBLANES, stride=seg)
        a = ab_ref[rows]
        hb = a * hb + bb_ref[rows]
        pb = pb * a
        bb_ref[rows] = hb
        ab_ref[rows] = pb
        return hf, pf, hb, pb

    zero = jnp.zeros((SUBLANES, LANES), F32)
    one = jnp.ones((SUBLANES, LANES), F32)
    hf, pf, hb, pb = lax.fori_loop(0, seg, scan_body, (zero, one, zero, one))

    carry_ref[0:8] = hf
    carry_ref[8:16] = pf
    carry_ref[16:24] = hb
    carry_ref[24:32] = pb
    state = jnp.zeros((1, LANES), F32)
    for s in range(SUBLANES):
        carry_ref[32 + s:33 + s] = state
        state = carry_ref[8 + s:9 + s] * state + carry_ref[s:s + 1]
    state = jnp.zeros((1, LANES), F32)
    for s in reversed(range(SUBLANES)):
        carry_ref[40 + s:41 + s] = state
        state = carry_ref[24 + s:25 + s] * state + carry_ref[16 + s:17 + s]
    cin_f = carry_ref[32:40]
    cin_b = carry_ref[40:48]

    def fix_body(t, c):
        rows = pl.ds(t, SUBLANES, stride=seg)
        bf_ref[rows] = (bf_ref[rows] + af_ref[rows] * cin_f) + (bb_ref[rows] + ab_ref[rows] * cin_b)
        return c

    lax.fori_loop(0, seg, fix_body, 0)

    def emit(rows, t0, n):
        o_ref[rows] = (jax.nn.gelu(gate_ref[rows]) * bf_ref[pl.ds(t0, n)]).astype(BF16)

    _for_storage_chunks(L, emit)


def _padfill_kernel(o_ref):
    o_ref[...] = jnp.zeros_like(o_ref)


def _padfill_call(T, D, gaps):
    (g0, n0), (g1, n1) = gaps
    nb0, nb1 = n0 // BF16_SUBLANES, n1 // BF16_SUBLANES
    b0, b1 = g0 // BF16_SUBLANES, g1 // BF16_SUBLANES
    return pl.pallas_call(
        _padfill_kernel,
        grid=(nb0 + nb1,),
        out_specs=pl.BlockSpec((BF16_SUBLANES, D), lambda i: (jnp.where(i < nb0, b0 + i, b1 + i - nb0), 0)),
        out_shape=jax.ShapeDtypeStruct((T, D), BF16),
        name="mix_pad",
    )()


def _mixer_params(L, nseq, nblk, scratch_rows_bytes):
    return dict(
        grid=(nseq, nblk),
        compiler_params=pltpu.CompilerParams(
            dimension_semantics=("parallel", "arbitrary"),
            vmem_limit_bytes=_vmem_limit(scratch_rows_bytes)),
    )


def _conv_call(proj, y, conv_w, conv_b, layer, L, nseq, blk0, dconv):
    nb = dconv // LANES
    T, D = y.shape
    seq = lambda base: pl.BlockSpec((L, LANES), lambda b, j: (b + blk0, base + j))
    return pl.pallas_call(
        functools.partial(_conv_kernel, L=L),
        in_specs=[seq(0), seq(nb), seq(2 * nb),
                  pl.BlockSpec((None, SHORT_CONV_WIDTH, LANES), lambda b, j: (layer, 0, j)),
                  pl.BlockSpec((None, 1, LANES), lambda b, j: (layer, 0, j)),
                  pl.BlockSpec(memory_space=pl.ANY)],
        out_specs=pl.BlockSpec((L, LANES), lambda b, j: (b + blk0, j)),
        out_shape=jax.ShapeDtypeStruct((T, D), BF16),
        scratch_shapes=[pltpu.VMEM((L + 2 * MARGIN, LANES), F32)],
        input_output_aliases={5: 0},
        name="mix_conv",
        **_mixer_params(L, nseq, nb, 9 * L * LANES * 4),
    )(proj, proj, proj, conv_w, conv_b, y)


def _lru_call(proj, y, cw, cb, wg, bg, lam, layer, L, nseq, blk0, col0, out0, nheads):
    T, D = y.shape
    c0 = col0 // LANES
    return pl.pallas_call(
        functools.partial(_lru_kernel, L=L),
        in_specs=[pl.BlockSpec((L, LANES), lambda b, j: (b + blk0, c0 + j)),
                  pl.BlockSpec((L, LANES), lambda b, j: (b + blk0, c0 + nheads + j)),
                  pl.BlockSpec((None, 2, LRU_CONV_WIDTH, LANES), lambda b, j: (layer, 0, 0, j)),
                  pl.BlockSpec((None, 2, LANES), lambda b, j: (layer, 0, j)),
                  pl.BlockSpec((None, None, 2, LANES, 2 * LANES), lambda b, j: (layer, j, 0, 0, 0)),
                  pl.BlockSpec((None, None, 2, 2 * LANES), lambda b, j: (layer, j, 0, 0)),
                  pl.BlockSpec((None, 2, LANES), lambda b, j: (layer, 0, j)),
                  pl.BlockSpec(memory_space=pl.ANY)],
        out_specs=pl.BlockSpec((L, LANES), lambda b, j: (b + blk0, out0 // LANES + j)),
        out_shape=jax.ShapeDtypeStruct((T, D), BF16),
        scratch_shapes=[pltpu.VMEM((L + 2 * MARGIN, LANES), F32)]
        + [pltpu.VMEM((L, LANES), F32)] * 4
        + [pltpu.VMEM((6 * SUBLANES, LANES), F32)],
        input_output_aliases={7: 0},
        name="mix_lru",
        **_mixer_params(L, nseq, nheads, 11 * L * LANES * 4),
    )(proj, proj, cw, cb, wg, bg, lam, y)


def _pool_call(proj, y, pool_w, pool_scale, layer, L, nseq, blk0, col0, out0):
    T, D = y.shape
    ng = len(POOL_WINDOWS)
    return pl.pallas_call(
        functools.partial(_pool_kernel, L=L),
        in_specs=[pl.BlockSpec((L, LANES), lambda b, j: (b + blk0, col0 // LANES + j)),
                  pl.BlockSpec((None, None, LANES, LANES), lambda b, j: (layer, j, 0, 0)),
                  pl.BlockSpec((None, 1, LANES), lambda b, j: (layer, 0, j)),
                  pl.BlockSpec(memory_space=pl.ANY)],
        out_specs=pl.BlockSpec((L, LANES), lambda b, j: (b + blk0, out0 // LANES + j)),
        out_shape=jax.ShapeDtypeStruct((T, D), BF16),
        scratch_shapes=[pltpu.VMEM((L + 2 * MARGIN, LANES), F32)],
        input_output_aliases={3: 0},
        name="mix_pool",
        **_mixer_params(L, nseq, ng, 5 * L * LANES * 4),
    )(proj, pool_w, pool_scale, y)


def kernel(x_prompt, x_sample, meta_tokens, ln_in_g, ln_in_b, ffn1_w_gate, ffn1_w_up, ffn1_w_down, ln1_g, ln1_b, w_in, conv_w, conv_b, lru_conv_w, lru_conv_b, lru_w_a, lru_b_a, lru_w_x, lru_b_x, lru_lambda, pool_w, pool_scale, w_out, ln2_g, ln2_b, ffn2_w_gate, ffn2_w_up, ffn2_w_down, ln3_g, ln3_b):
    depth = ffn1_w_gate.shape[0]
    alpha = float((2 * depth) ** 0.25)
    bp, sp, D = x_prompt.shape
    bs, ss, _ = x_sample.shape
    lp, ls = sp + N_META, ss + N_META
    dconv = conv_w.shape[-1]
    dlru = lru_lambda.shape[-1]
    dpool = pool_scale.shape[-1]
    nheads = lru_w_a.shape[2]
    assert D == dconv + dlru + dpool and dlru // nheads == LANES and dpool // len(POOL_WINDOWS) == LANES
    assert dconv % LANES == 0 and sp % COPY_CHUNK == 0 and ss % COPY_CHUNK == 0
    assert lp % BF16_SUBLANES == 0 and ls % BF16_SUBLANES == 0

    ts = bs * ls
    blk0_p = -(-ts // lp)
    off_p = blk0_p * lp
    t_real = off_p + bp * lp
    t_pad = -(-t_real // ROW_TILE) * ROW_TILE
    gaps = ((ts, off_p - ts), (t_real, t_pad - t_real))

    def with_meta(x):
        meta = jnp.broadcast_to(meta_tokens[None], (x.shape[0], N_META, D))
        return jnp.concatenate([x, meta], axis=1).reshape(-1, D)

    h = jnp.concatenate([with_meta(x_sample), jnp.zeros((off_p - ts, D), F32),
                         with_meta(x_prompt), jnp.zeros((t_pad - t_real, D), F32)], axis=0)

    bf = lambda w: w.astype(BF16)
    row3 = lambda p: p.reshape(depth, 1, -1)
    f1 = (bf(ffn1_w_gate), bf(ffn1_w_up), bf(ffn1_w_down), row3(ln1_g), row3(ln1_b))
    f2 = (bf(ffn2_w_gate), bf(ffn2_w_up), bf(ffn2_w_down), row3(ln3_g), row3(ln3_b))
    w_in_b, w_out_b = bf(w_in), bf(w_out)
    gate_w = bf(jnp.concatenate([lru_w_a, lru_w_x], axis=-1)).transpose(0, 2, 1, 3, 4)
    gate_b = jnp.concatenate([lru_b_a.reshape(depth, 2, nheads, LANES),
                              lru_b_x.reshape(depth, 2, nheads, LANES)], axis=-1).transpose(0, 2, 1, 3)
    pool_w_b = bf(pool_w)
    conv_b3, pool_s3 = row3(conv_b), row3(pool_scale)
    ln2 = (row3(ln2_g), row3(ln2_b))

    lru_col, pool_col = 3 * dconv, 3 * dconv + 2 * dlru
    groups = ((ls, bs, 0), (lp, bp, blk0_p))

    h = _ln_call(h, ln_in_g, ln_in_b)
    for layer in range(depth):
        h = _ffn_call(h, *f1, layer, alpha)
        proj = _inproj_call(h, w_in_b, layer)
        y = _padfill_call(t_pad, D, gaps)
        for L, nseq, blk0 in groups:
            y = _conv_call(proj, y, conv_w, conv_b3, layer, L, nseq, blk0, dconv)
            y = _lru_call(proj, y, lru_conv_w, lru_conv_b, gate_w, gate_b, lru_lambda,
                          layer, L, nseq, blk0, lru_col, dconv, nheads)
            y = _pool_call(proj, y, pool_w_b, pool_s3, layer, L, nseq, blk0, pool_col, dconv + dlru)
        h = _outproj_call(h, y, w_out_b, *ln2, layer, alpha)
        h = _ffn_call(h, *f2, layer, alpha)

    y_sample = h[:ts].reshape(bs, ls, D)[:, :ss]
    y_prompt = h[off_p:t_real].reshape(bp, lp, D)[:, :sp]
    return (y_prompt, y_sample)
```

```python
import functools

import jax
import jax.numpy as jnp
from jax import lax
from jax.experimental import pallas as pl
from jax.experimental.pallas import tpu as pltpu

F32 = jnp.float32
BF16 = jnp.bfloat16

N_META = 16
POOL_WINDOWS = (2, 4, 8, 16)
SHORT_CONV_WIDTH = 3
LRU_CONV_WIDTH = 4
LRU_C = 8.0
LN_EPS = 1e-5

LANES = 128
SUBLANES = 8
BF16_SUBLANES = 16
VMEM_BYTES_V7X = 64 * 1024 * 1024

ROW_TILE = 512
FF_TILE = 512
PROJ_TILE = 1024
MIX_CHUNK = 128
COPY_CHUNK = 256
MARGIN = SUBLANES
SCAN_SEGMENTS = 2 * SUBLANES


def _vmem_limit(block_bytes):
    return int(min(block_bytes + block_bytes // 2 + (8 << 20), VMEM_BYTES_V7X - (6 << 20)))


def _layernorm(z, g, b):
    mu = jnp.mean(z, axis=-1, keepdims=True)
    d = z - mu
    var = jnp.mean(d * d, axis=-1, keepdims=True)
    return d * lax.rsqrt(var + LN_EPS) * g + b


def _fill_call(T, D, gaps, dtype):
    (g0, n0), (g1, n1) = gaps
    nb0, nb1 = n0 // BF16_SUBLANES, n1 // BF16_SUBLANES
    b0, b1 = g0 // BF16_SUBLANES, g1 // BF16_SUBLANES

    def fill_kernel(o_ref):
        o_ref[...] = jnp.zeros_like(o_ref)

    return pl.pallas_call(
        fill_kernel,
        grid=(nb0 + nb1,),
        out_specs=pl.BlockSpec((BF16_SUBLANES, D), lambda i: (jnp.where(i < nb0, b0 + i, b1 + i - nb0), 0)),
        out_shape=jax.ShapeDtypeStruct((T, D), dtype),
        name="gap_fill",
    )()


def _ln_kernel(x_ref, g_ref, b_ref, hprev_ref, o_ref):
    del hprev_ref
    o_ref[...] = _layernorm(x_ref[...], g_ref[...], b_ref[...])


def _ln_in_call(x, h, g, b, L, blk0, rows_per_step, row0):
    nseq, S, D = x.shape
    T = h.shape[0]
    nj = S // rows_per_step
    return pl.pallas_call(
        _ln_kernel,
        grid=(nseq, nj),
        in_specs=[pl.BlockSpec((None, rows_per_step, D), lambda s, j: (s, j, 0)),
                  pl.BlockSpec((1, D), lambda s, j: (0, 0)),
                  pl.BlockSpec((1, D), lambda s, j: (0, 0)),
                  pl.BlockSpec(memory_space=pl.ANY)],
        out_specs=pl.BlockSpec((pl.Element(rows_per_step), pl.Element(D)),
                               lambda s, j: (pl.multiple_of((blk0 + s) * L + row0 + j * rows_per_step,
                                                            SUBLANES), 0)),
        out_shape=jax.ShapeDtypeStruct((T, D), F32),
        input_output_aliases={3: 0},
        compiler_params=pltpu.CompilerParams(
            dimension_semantics=("arbitrary", "arbitrary"),
            vmem_limit_bytes=_vmem_limit(4 * rows_per_step * D * 4)),
        name="ln_in",
    )(x, g.reshape(1, D), b.reshape(1, D), h)


def _ffn_kernel(x_ref, wg_ref, wu_ref, wd_ref, g_ref, b_ref, o_ref, xb_ref, acc_ref, *, nk, alpha):
    k = pl.program_id(1)

    @pl.when(k == 0)
    def _():
        xb_ref[...] = x_ref[...].astype(BF16)
        acc_ref[...] = jnp.zeros_like(acc_ref)

    xb = xb_ref[...]
    gate = jnp.dot(xb, wg_ref[...], preferred_element_type=F32)
    up = jnp.dot(xb, wu_ref[...], preferred_element_type=F32)
    hidden = (gate * jax.nn.sigmoid(gate) * up).astype(BF16)
    acc_ref[...] += jnp.dot(hidden, wd_ref[...], preferred_element_type=F32)

    @pl.when(k == nk - 1)
    def _():
        z = alpha * x_ref[...] + 0.5 * acc_ref[...]
        o_ref[...] = _layernorm(z, g_ref[...], b_ref[...])


def _ffn_call(h, wg, wu, wd, g, b, layer, alpha, seq=None):
    T, D = h.shape
    F = wg.shape[-1]
    nk = F // FF_TILE
    if seq is None:
        n_tiles, out_rows = T // ROW_TILE, T
        x_spec = pl.BlockSpec((ROW_TILE, D), lambda i, k: (i, 0))
    else:
        L, nseq, blk0, S = seq
        nj = S // ROW_TILE
        n_tiles, out_rows = nseq * nj, nseq * S
        x_spec = pl.BlockSpec((pl.Element(ROW_TILE), pl.Element(D)),
                              lambda i, k: (pl.multiple_of((blk0 + i // nj) * L + (i % nj) * ROW_TILE,
                                                           SUBLANES), 0))
    blocks = (4 * ROW_TILE * D * 4 + 6 * D * FF_TILE * 2 + ROW_TILE * D * (2 + 4)
              + 4 * ROW_TILE * FF_TILE * 4)
    return pl.pallas_call(
        functools.partial(_ffn_kernel, nk=nk, alpha=alpha),
        grid=(n_tiles, nk),
        in_specs=[x_spec,
                  pl.BlockSpec((None, D, FF_TILE), lambda i, k: (layer, 0, k)),
                  pl.BlockSpec((None, D, FF_TILE), lambda i, k: (layer, 0, k)),
                  pl.BlockSpec((None, FF_TILE, D), lambda i, k: (layer, k, 0)),
                  pl.BlockSpec((None, 1, D), lambda i, k: (layer, 0, 0)),
                  pl.BlockSpec((None, 1, D), lambda i, k: (layer, 0, 0))],
        out_specs=pl.BlockSpec((ROW_TILE, D), lambda i, k: (i, 0)),
        out_shape=jax.ShapeDtypeStruct((out_rows, D), F32),
        scratch_shapes=[pltpu.VMEM((ROW_TILE, D), BF16), pltpu.VMEM((ROW_TILE, D), F32)],
        compiler_params=pltpu.CompilerParams(
            dimension_semantics=("parallel", "arbitrary"),
            vmem_limit_bytes=_vmem_limit(blocks)),
        name="ffn",
    )(h, wg, wu, wd, g, b)


def _inproj_kernel(x_ref, w_ref, o_ref):
    xb = x_ref[...].astype(BF16)
    for n in range(o_ref.shape[-1] // PROJ_TILE):
        cols = slice(n * PROJ_TILE, (n + 1) * PROJ_TILE)
        o_ref[:, cols] = jnp.dot(xb, w_ref[:, cols], preferred_element_type=F32)


def _inproj_call(h, w_in, layer):
    T, D = h.shape
    P = w_in.shape[-1]
    blocks = 2 * ROW_TILE * D * 4 + D * P * 2 + 2 * ROW_TILE * P * 4 + ROW_TILE * D * 2
    return pl.pallas_call(
        _inproj_kernel,
        grid=(T // ROW_TILE,),
        in_specs=[pl.BlockSpec((ROW_TILE, D), lambda i: (i, 0)),
                  pl.BlockSpec((None, D, P), lambda i: (layer, 0, 0), pipeline_mode=pl.Buffered(1))],
        out_specs=pl.BlockSpec((ROW_TILE, P), lambda i: (i, 0)),
        out_shape=jax.ShapeDtypeStruct((T, P), F32),
        compiler_params=pltpu.CompilerParams(
            dimension_semantics=("parallel",),
            vmem_limit_bytes=_vmem_limit(blocks)),
        name="in_proj",
    )(h, w_in)


def _outproj_kernel(h_ref, y_ref, w_ref, g_ref, b_ref, o_ref, *, alpha):
    mix = jnp.dot(y_ref[...], w_ref[...], preferred_element_type=F32)
    o_ref[...] = _layernorm(alpha * h_ref[...] + mix, g_ref[...], b_ref[...])


def _outproj_call(h, y, w_out, g, b, layer, alpha):
    T, D = h.shape
    blocks = 4 * ROW_TILE * D * 4 + 2 * ROW_TILE * D * 2 + D * D * 2 + 2 * ROW_TILE * D * 4
    return pl.pallas_call(
        functools.partial(_outproj_kernel, alpha=alpha),
        grid=(T // ROW_TILE,),
        in_specs=[pl.BlockSpec((ROW_TILE, D), lambda i: (i, 0)),
                  pl.BlockSpec((ROW_TILE, D), lambda i: (i, 0)),
                  pl.BlockSpec((None, D, D), lambda i: (layer, 0, 0), pipeline_mode=pl.Buffered(1)),
                  pl.BlockSpec((None, 1, D), lambda i: (layer, 0, 0)),
                  pl.BlockSpec((None, 1, D), lambda i: (layer, 0, 0))],
        out_specs=pl.BlockSpec((ROW_TILE, D), lambda i: (i, 0)),
        out_shape=jax.ShapeDtypeStruct((T, D), F32),
        compiler_params=pltpu.CompilerParams(
            dimension_semantics=("parallel",),
            vmem_limit_bytes=_vmem_limit(blocks)),
        name="out_proj",
    )(h, y, w_out, g, b)


def _fill_padded(dst_ref, L, read):
    lx = L - N_META
    zeros = jnp.zeros((MARGIN, LANES), F32)
    dst_ref[0:MARGIN] = zeros
    dst_ref[MARGIN + L:2 * MARGIN + L] = zeros
    dst_ref[MARGIN:MARGIN + N_META] = read(pl.ds(lx, N_META))

    def body(i, c):
        r = pl.multiple_of(i * COPY_CHUNK, COPY_CHUNK)
        dst = pl.multiple_of(r + (MARGIN + N_META), SUBLANES)
        dst_ref[pl.ds(dst, COPY_CHUNK)] = read(pl.ds(r, COPY_CHUNK))
        return c

    lax.fori_loop(0, lx // COPY_CHUNK, body, 0)


def _for_storage_chunks(L, fn):
    lx = L - N_META

    def body(i, c):
        r = pl.multiple_of(i * COPY_CHUNK, COPY_CHUNK)
        fn(pl.ds(r, COPY_CHUNK), r + N_META, COPY_CHUNK)
        return c

    lax.fori_loop(0, lx // COPY_CHUNK, body, 0)
    fn(pl.ds(lx, N_META), 0, N_META)


def _for_logical_chunks(L, chunk, fn):
    n_main = L // chunk

    def body(i, c):
        fn(pl.multiple_of(i * chunk, chunk), chunk)
        return c

    lax.fori_loop(0, n_main, body, 0)
    if L % chunk:
        fn(n_main * chunk, L % chunk)


def _conv_kernel(bg_ref, cg_ref, v_ref, w_ref, cb_ref, yprev_ref, o_ref, upad_ref, *, L):
    del yprev_ref
    _fill_padded(upad_ref, L, lambda rows: cg_ref[rows] * v_ref[rows])
    w = w_ref[...]
    cb = cb_ref[...]

    def emit(rows, t0, n):
        acc = cb
        for k in range(SHORT_CONV_WIDTH):
            off = k - SHORT_CONV_WIDTH // 2
            acc = acc + w[k:k + 1] * upad_ref[pl.ds(t0 + (MARGIN + off), n)]
        o_ref[rows] = (bg_ref[rows] * acc).astype(BF16)

    _for_storage_chunks(L, emit)


def _pool_kernel(u_ref, pw_ref, ps_ref, yprev_ref, o_ref, upad_ref, *, L):
    del yprev_ref
    _fill_padded(upad_ref, L, lambda rows: u_ref[rows])
    pw = pw_ref[...]
    ps = ps_ref[...]
    group = pl.program_id(1)

    for gi, win in enumerate(POOL_WINDOWS):
        half = win // 2

        @pl.when(group == gi)
        def _(win=win, half=half):
            def emit(rows, t0, n):
                total = upad_ref[pl.ds(t0 + (MARGIN - half), n)]
                for j in range(1 - half, half):
                    total = total + upad_ref[pl.ds(t0 + (MARGIN + j), n)]
                t = t0 + lax.broadcasted_iota(jnp.int32, (n, LANES), 0)
                lo = jnp.maximum(t - half, 0)
                hi = jnp.minimum(t + (half - 1), L - 1)
                count = (hi - lo + 1).astype(F32)
                pooled = total / count - upad_ref[pl.ds(t0 + MARGIN, n)]
                mixed = jnp.dot(pooled.astype(BF16), pw, preferred_element_type=F32)
                o_ref[rows] = (mixed * ps).astype(BF16)

            _for_storage_chunks(L, emit)


def _lru_kernel(x_ref, gate_ref, cw_ref, cb_ref, wg_ref, bg_ref, lam_ref, yprev_ref, o_ref,
                xpad_ref, xc_ref, a_ref, b_ref, h_ref, pre0_ref, pre1_ref, carry_ref, *, L):
    del yprev_ref
    seg = L // SCAN_SEGMENTS
    _fill_padded(xpad_ref, L, lambda rows: x_ref[rows])

    cw = cw_ref[...]
    cb = cb_ref[...]
    bg = bg_ref[...]
    neg_lam = -lam_ref[...]
    softplus = jnp.maximum(neg_lam, 0.0) + jnp.log1p(jnp.exp(-jnp.abs(neg_lam)))
    half_c = (-0.5 * LRU_C) * softplus

    def conv(t0, n):
        for d in range(2):
            xc = cb[d:d + 1]
            for k in range(LRU_CONV_WIDTH):
                off = (k - (LRU_CONV_WIDTH - 1)) if d == 0 else ((LRU_CONV_WIDTH - 1) - k)
                xc = xc + cw[d, k:k + 1] * xpad_ref[pl.ds(t0 + (MARGIN + off), n)]
            xc_ref[d, pl.ds(t0, n)] = xc

    _for_logical_chunks(L, MIX_CHUNK, conv)

    def gate_matmuls(t0, n, pre_ref):
        for d in range(2):
            pre_ref[d, 0:n] = jnp.dot(xc_ref[d, pl.ds(t0, n)].astype(BF16), wg_ref[d],
                                      preferred_element_type=F32)

    def finish(t0, n, pre_ref):
        for d in range(2):
            pre = pre_ref[d, 0:n] + bg[d:d + 1]
            t_r = jnp.tanh(0.5 * pre[:, :LANES])
            t_i = jnp.tanh(0.5 * pre[:, LANES:])
            hc = half_c[d:d + 1]
            a = jnp.exp(hc + hc * t_r)
            gated = (0.5 * xc_ref[d, pl.ds(t0, n)]) * (1.0 + t_i)
            a_ref[d, pl.ds(t0, n)] = a
            b_ref[d, pl.ds(t0, n)] = jnp.sqrt(1.0 - a * a) * gated

    n_main = L // MIX_CHUNK
    assert n_main % 2 == 0
    gate_matmuls(0, MIX_CHUNK, pre0_ref)

    def gates_body(i, c):
        c0 = pl.multiple_of(2 * i * MIX_CHUNK, MIX_CHUNK)
        c1 = pl.multiple_of(c0 + MIX_CHUNK, MIX_CHUNK)
        nxt = pl.multiple_of(jnp.minimum(2 * i + 2, n_main - 1) * MIX_CHUNK, MIX_CHUNK)
        gate_matmuls(c1, MIX_CHUNK, pre1_ref)
        finish(c0, MIX_CHUNK, pre0_ref)
        gate_matmuls(nxt, MIX_CHUNK, pre0_ref)
        finish(c1, MIX_CHUNK, pre1_ref)
        return c

    lax.fori_loop(0, n_main // 2, gates_body, 0)
    if L % MIX_CHUNK:
        t0, n = n_main * MIX_CHUNK, L % MIX_CHUNK
        gate_matmuls(t0, n, pre0_ref)
        finish(t0, n, pre0_ref)

    nv = SCAN_SEGMENTS // SUBLANES

    def seg_rows(j, t):
        return pl.ds(j * SUBLANES * seg + t, SUBLANES, stride=seg)

    def ends_body(t, c):
        out = []
        for d in range(2):
            tt = t if d == 0 else seg - 1 - t
            for j in range(nv):
                h, p = c[2 * (d * nv + j)], c[2 * (d * nv + j) + 1]
                rows = seg_rows(j, tt)
                a = a_ref[d, rows]
                out += [a * h + b_ref[d, rows], p * a]
        return tuple(out)

    zero = jnp.zeros((SUBLANES, LANES), F32)
    one = jnp.ones((SUBLANES, LANES), F32)
    ends = lax.fori_loop(0, seg, ends_body, (zero, one) * (2 * nv), unroll=4)

    S = SCAN_SEGMENTS
    for d in range(2):
        for j in range(nv):
            carry_ref[d, pl.ds(j * SUBLANES, SUBLANES)] = ends[2 * (d * nv + j)]
            carry_ref[d, pl.ds(S + j * SUBLANES, SUBLANES)] = ends[2 * (d * nv + j) + 1]
        state = jnp.zeros((1, LANES), F32)
        for s in (range(S) if d == 0 else reversed(range(S))):
            carry_ref[d, 2 * S + s:2 * S + s + 1] = state
            state = carry_ref[d, S + s:S + s + 1] * state + carry_ref[d, s:s + 1]
    cin = tuple(carry_ref[d, pl.ds(2 * S + j * SUBLANES, SUBLANES)] for d in range(2) for j in range(nv))

    def scan_body(t, c):
        out = []
        for d in range(2):
            tt = t if d == 0 else seg - 1 - t
            for j in range(nv):
                rows = seg_rows(j, tt)
                h = a_ref[d, rows] * c[d * nv + j] + b_ref[d, rows]
                h_ref[d, rows] = h
                out.append(h)
        return tuple(out)

    lax.fori_loop(0, seg, scan_body, cin, unroll=4)

    def emit(rows, t0, n):
        hsum = h_ref[0, pl.ds(t0, n)] + h_ref[1, pl.ds(t0, n)]
        o_ref[rows] = (jax.nn.gelu(gate_ref[rows]) * hsum).astype(BF16)

    _for_storage_chunks(L, emit)


def _mixer_params(nseq, nblk, scratch_bytes):
    return dict(
        grid=(nseq, nblk),
        compiler_params=pltpu.CompilerParams(
            dimension_semantics=("parallel", "arbitrary"),
            vmem_limit_bytes=_vmem_limit(scratch_bytes)),
    )


def _conv_call(proj, y, conv_w, conv_b, layer, L, nseq, blk0, dconv):
    nb = dconv // LANES
    T, D = y.shape
    seq = lambda base: pl.BlockSpec((L, LANES), lambda b, j: (b + blk0, base + j))
    return pl.pallas_call(
        functools.partial(_conv_kernel, L=L),
        in_specs=[seq(0), seq(nb), seq(2 * nb),
                  pl.BlockSpec((None, SHORT_CONV_WIDTH, LANES), lambda b, j: (layer, 0, j)),
                  pl.BlockSpec((None, 1, LANES), lambda b, j: (layer, 0, j)),
                  pl.BlockSpec(memory_space=pl.ANY)],
        out_specs=pl.BlockSpec((L, LANES), lambda b, j: (b + blk0, j)),
        out_shape=jax.ShapeDtypeStruct((T, D), BF16),
        scratch_shapes=[pltpu.VMEM((L + 2 * MARGIN, LANES), F32)],
        input_output_aliases={5: 0},
        name="mix_conv",
        **_mixer_params(nseq, nb, 9 * L * LANES * 4),
    )(proj, proj, proj, conv_w, conv_b, y)


def _lru_call(proj, y, cw, cb, wg, bg, lam, layer, L, nseq, blk0, col0, out0, nheads):
    T, D = y.shape
    c0 = col0 // LANES
    return pl.pallas_call(
        functools.partial(_lru_kernel, L=L),
        in_specs=[pl.BlockSpec((L, LANES), lambda b, j: (b + blk0, c0 + j)),
                  pl.BlockSpec((L, LANES), lambda b, j: (b + blk0, c0 + nheads + j)),
                  pl.BlockSpec((None, 2, LRU_CONV_WIDTH, LANES), lambda b, j: (layer, 0, 0, j)),
                  pl.BlockSpec((None, 2, LANES), lambda b, j: (layer, 0, j)),
                  pl.BlockSpec((None, None, 2, LANES, 2 * LANES), lambda b, j: (layer, j, 0, 0, 0)),
                  pl.BlockSpec((None, None, 2, 2 * LANES), lambda b, j: (layer, j, 0, 0)),
                  pl.BlockSpec((None, 2, LANES), lambda b, j: (layer, 0, j)),
                  pl.BlockSpec(memory_space=pl.ANY)],
        out_specs=pl.BlockSpec((L, LANES), lambda b, j: (b + blk0, out0 // LANES + j)),
        out_shape=jax.ShapeDtypeStruct((T, D), BF16),
        scratch_shapes=[pltpu.VMEM((L + 2 * MARGIN, LANES), F32)]
        + [pltpu.VMEM((2, L, LANES), F32)] * 4
        + [pltpu.VMEM((2, MIX_CHUNK, 2 * LANES), F32)] * 2
        + [pltpu.VMEM((2, 3 * SCAN_SEGMENTS, LANES), F32)],
        input_output_aliases={7: 0},
        name="mix_lru",
        **_mixer_params(nseq, nheads, 15 * L * LANES * 4),
    )(proj, proj, cw, cb, wg, bg, lam, y)


def _pool_call(proj, y, pool_w, pool_scale, layer, L, nseq, blk0, col0, out0):
    T, D = y.shape
    ng = len(POOL_WINDOWS)
    return pl.pallas_call(
        functools.partial(_pool_kernel, L=L),
        in_specs=[pl.BlockSpec((L, LANES), lambda b, j: (b + blk0, col0 // LANES + j)),
                  pl.BlockSpec((None, None, LANES, LANES), lambda b, j: (layer, j, 0, 0)),
                  pl.BlockSpec((None, 1, LANES), lambda b, j: (layer, 0, j)),
                  pl.BlockSpec(memory_space=pl.ANY)],
        out_specs=pl.BlockSpec((L, LANES), lambda b, j: (b + blk0, out0 // LANES + j)),
        out_shape=jax.ShapeDtypeStruct((T, D), BF16),
        scratch_shapes=[pltpu.VMEM((L + 2 * MARGIN, LANES), F32)],
        input_output_aliases={3: 0},
        name="mix_pool",
        **_mixer_params(nseq, ng, 5 * L * LANES * 4),
    )(proj, pool_w, pool_scale, y)


def kernel(x_prompt, x_sample, meta_tokens, ln_in_g, ln_in_b, ffn1_w_gate, ffn1_w_up, ffn1_w_down, ln1_g, ln1_b, w_in, conv_w, conv_b, lru_conv_w, lru_conv_b, lru_w_a, lru_b_a, lru_w_x, lru_b_x, lru_lambda, pool_w, pool_scale, w_out, ln2_g, ln2_b, ffn2_w_gate, ffn2_w_up, ffn2_w_down, ln3_g, ln3_b):
    depth = ffn1_w_gate.shape[0]
    alpha = float((2 * depth) ** 0.25)
    bp, sp, D = x_prompt.shape
    bs, ss, _ = x_sample.shape
    lp, ls = sp + N_META, ss + N_META
    dconv = conv_w.shape[-1]
    dlru = lru_lambda.shape[-1]
    dpool = pool_scale.shape[-1]
    nheads = lru_w_a.shape[2]
    assert D == dconv + dlru + dpool and dlru // nheads == LANES and dpool // len(POOL_WINDOWS) == LANES
    assert dconv % LANES == 0 and sp % ROW_TILE == 0 and ss % ROW_TILE == 0
    assert lp % SCAN_SEGMENTS == 0 and ls % SCAN_SEGMENTS == 0

    ts = bs * ls
    blk0_p = -(-ts // lp)
    off_p = blk0_p * lp
    t_real = off_p + bp * lp
    t_pad = -(-t_real // ROW_TILE) * ROW_TILE
    gaps = ((ts, off_p - ts), (t_real, t_pad - t_real))
    groups = ((x_sample, ls, bs, 0, ss), (x_prompt, lp, bp, blk0_p, sp))

    bf = lambda w: w.astype(BF16)
    row3 = lambda p: p.reshape(depth, 1, -1)
    f1 = (bf(ffn1_w_gate), bf(ffn1_w_up), bf(ffn1_w_down), row3(ln1_g), row3(ln1_b))
    f2 = (bf(ffn2_w_gate), bf(ffn2_w_up), bf(ffn2_w_down), row3(ln3_g), row3(ln3_b))
    w_in_b, w_out_b = bf(w_in), bf(w_out)
    gate_w = bf(jnp.concatenate([lru_w_a, lru_w_x], axis=-1)).transpose(0, 2, 1, 3, 4)
    gate_b = jnp.concatenate([lru_b_a.reshape(depth, 2, nheads, LANES),
                              lru_b_x.reshape(depth, 2, nheads, LANES)], axis=-1).transpose(0, 2, 1, 3)
    pool_w_b = bf(pool_w)
    conv_b3, pool_s3 = row3(conv_b), row3(pool_scale)
    ln2 = (row3(ln2_g), row3(ln2_b))
    lru_col, pool_col = 3 * dconv, 3 * dconv + 2 * dlru

    h = _fill_call(t_pad, D, gaps, F32)
    for x, L, nseq, blk0, S in groups:
        h = _ln_in_call(x, h, ln_in_g, ln_in_b, L, blk0, ROW_TILE, 0)
        meta = jnp.broadcast_to(meta_tokens[None], (nseq, N_META, D))
        h = _ln_in_call(meta, h, ln_in_g, ln_in_b, L, blk0, N_META, S)

    outs = []
    for layer in range(depth):
        h = _ffn_call(h, *f1, layer, alpha)
        proj = _inproj_call(h, w_in_b, layer)
        y = _fill_call(t_pad, D, gaps, BF16)
        for _, L, nseq, blk0, _ in groups:
            y = _conv_call(proj, y, conv_w, conv_b3, layer, L, nseq, blk0, dconv)
            y = _lru_call(proj, y, lru_conv_w, lru_conv_b, gate_w, gate_b, lru_lambda,
                          layer, L, nseq, blk0, lru_col, dconv, nheads)
            y = _pool_call(proj, y, pool_w_b, pool_s3, layer, L, nseq, blk0, pool_col, dconv + dlru)
        h = _outproj_call(h, y, w_out_b, *ln2, layer, alpha)
        if layer + 1 < depth:
            h = _ffn_call(h, *f2, layer, alpha)
        else:
            outs = [_ffn_call(h, *f2, layer, alpha, seq=(L, nseq, blk0, S)).reshape(nseq, S, D)
                    for _, L, nseq, blk0, S in groups]

    y_sample, y_prompt = outs
    return (y_prompt, y_sample)
```

```python
import functools

import jax
import jax.numpy as jnp
from jax import lax
from jax.experimental import pallas as pl
from jax.experimental.pallas import tpu as pltpu

F32 = jnp.float32
BF16 = jnp.bfloat16

N_META = 16
POOL_WINDOWS = (2, 4, 8, 16)
SHORT_CONV_WIDTH = 3
LRU_CONV_WIDTH = 4
LRU_C = 8.0
LN_EPS = 1e-5
LOG2_E = 1.4426950408889634
GELU_C0 = 0.7978845608028654
GELU_C1 = 0.044715

LANES = 128
SUBLANES = 8
BF16_SUBLANES = 16
VMEM_BYTES_V7X = 64 * 1024 * 1024

ROW_TILE = 512
FF_TILE = 512
PROJ_TILE = 1024
MIX_CHUNK = 256
COPY_CHUNK = 256
MARGIN = SUBLANES
SCAN_SEGMENTS = 2 * SUBLANES


def _vmem_limit(block_bytes):
    return int(min(block_bytes + block_bytes // 2 + (8 << 20), VMEM_BYTES_V7X - (6 << 20)))


def _layernorm(z, g, b):
    mu = jnp.mean(z, axis=-1, keepdims=True)
    d = z - mu
    var = jnp.mean(d * d, axis=-1, keepdims=True)
    return d * lax.rsqrt(var + LN_EPS) * g + b


def _ln_kernel(x_ref, g_ref, b_ref, hprev_ref, o_ref):
    del hprev_ref
    o_ref[...] = _layernorm(x_ref[...], g_ref[...], b_ref[...])


def _ln_in_call(x, h, g, b, L, blk0, rows_per_step, row0):
    nseq, S, D = x.shape
    T = h.shape[0]
    nj = S // rows_per_step
    return pl.pallas_call(
        _ln_kernel,
        grid=(nseq, nj),
        in_specs=[pl.BlockSpec((None, rows_per_step, D), lambda s, j: (s, j, 0)),
                  pl.BlockSpec((1, D), lambda s, j: (0, 0)),
                  pl.BlockSpec((1, D), lambda s, j: (0, 0)),
                  pl.BlockSpec(memory_space=pl.ANY)],
        out_specs=pl.BlockSpec((pl.Element(rows_per_step), pl.Element(D)),
                               lambda s, j: (pl.multiple_of((blk0 + s) * L + row0 + j * rows_per_step,
                                                            SUBLANES), 0)),
        out_shape=jax.ShapeDtypeStruct((T, D), F32),
        input_output_aliases={3: 0},
        compiler_params=pltpu.CompilerParams(
            dimension_semantics=("arbitrary", "arbitrary"),
            vmem_limit_bytes=_vmem_limit(4 * rows_per_step * D * 4)),
        name="ln_in",
    )(x, g.reshape(1, D), b.reshape(1, D), h)


def _ffn_kernel(x_ref, wg_ref, wu_ref, wd_ref, g_ref, b_ref, o_ref, xb_ref, acc_ref, *, nk, alpha):
    k = pl.program_id(1)

    @pl.when(k == 0)
    def _():
        xb_ref[...] = x_ref[...].astype(BF16)
        acc_ref[...] = jnp.zeros_like(acc_ref)

    xb = xb_ref[...]
    gate = jnp.dot(xb, wg_ref[...], preferred_element_type=F32)
    up = jnp.dot(xb, wu_ref[...], preferred_element_type=F32)
    hidden = (gate * jax.nn.sigmoid(gate) * up).astype(BF16)
    acc_ref[...] += jnp.dot(hidden, wd_ref[...], preferred_element_type=F32)

    @pl.when(k == nk - 1)
    def _():
        z = alpha * x_ref[...] + 0.5 * acc_ref[...]
        o_ref[...] = _layernorm(z, g_ref[...], b_ref[...])


def _ffn_call(h, wg, wu, wd, g, b, layer, alpha, seq=None):
    T, D = h.shape
    F = wg.shape[-1]
    nk = F // FF_TILE
    if seq is None:
        n_tiles, out_rows = T // ROW_TILE, T
        x_spec = pl.BlockSpec((ROW_TILE, D), lambda i, k: (i, 0))
    else:
        L, nseq, blk0, S = seq
        nj = S // ROW_TILE
        n_tiles, out_rows = nseq * nj, nseq * S
        x_spec = pl.BlockSpec((pl.Element(ROW_TILE), pl.Element(D)),
                              lambda i, k: (pl.multiple_of((blk0 + i // nj) * L + (i % nj) * ROW_TILE,
                                                           SUBLANES), 0))
    blocks = (4 * ROW_TILE * D * 4 + 6 * D * FF_TILE * 2 + ROW_TILE * D * (2 + 4)
              + 4 * ROW_TILE * FF_TILE * 4)
    return pl.pallas_call(
        functools.partial(_ffn_kernel, nk=nk, alpha=alpha),
        grid=(n_tiles, nk),
        in_specs=[x_spec,
                  pl.BlockSpec((None, D, FF_TILE), lambda i, k: (layer, 0, k)),
                  pl.BlockSpec((None, D, FF_TILE), lambda i, k: (layer, 0, k)),
                  pl.BlockSpec((None, FF_TILE, D), lambda i, k: (layer, k, 0)),
                  pl.BlockSpec((None, 1, D), lambda i, k: (layer, 0, 0)),
                  pl.BlockSpec((None, 1, D), lambda i, k: (layer, 0, 0))],
        out_specs=pl.BlockSpec((ROW_TILE, D), lambda i, k: (i, 0)),
        out_shape=jax.ShapeDtypeStruct((out_rows, D), F32),
        scratch_shapes=[pltpu.VMEM((ROW_TILE, D), BF16), pltpu.VMEM((ROW_TILE, D), F32)],
        compiler_params=pltpu.CompilerParams(
            dimension_semantics=("parallel", "arbitrary"),
            vmem_limit_bytes=_vmem_limit(blocks)),
        name="ffn",
    )(h, wg, wu, wd, g, b)


def _inproj_kernel(x_ref, w_ref, o_ref):
    xb = x_ref[...].astype(BF16)
    for n in range(o_ref.shape[-1] // PROJ_TILE):
        cols = slice(n * PROJ_TILE, (n + 1) * PROJ_TILE)
        o_ref[:, cols] = jnp.dot(xb, w_ref[:, cols], preferred_element_type=F32)


def _inproj_call(h, w_in, layer):
    T, D = h.shape
    P = w_in.shape[-1]
    blocks = 2 * ROW_TILE * D * 4 + D * P * 2 + 2 * ROW_TILE * P * 4 + ROW_TILE * D * 2
    return pl.pallas_call(
        _inproj_kernel,
        grid=(T // ROW_TILE,),
        in_specs=[pl.BlockSpec((ROW_TILE, D), lambda i: (i, 0)),
                  pl.BlockSpec((None, D, P), lambda i: (layer, 0, 0), pipeline_mode=pl.Buffered(1))],
        out_specs=pl.BlockSpec((ROW_TILE, P), lambda i: (i, 0)),
        out_shape=jax.ShapeDtypeStruct((T, P), F32),
        compiler_params=pltpu.CompilerParams(
            dimension_semantics=("parallel",),
            vmem_limit_bytes=_vmem_limit(blocks)),
        name="in_proj",
    )(h, w_in)


def _outproj_kernel(h_ref, y_ref, w_ref, g_ref, b_ref, o_ref, *, alpha):
    mix = jnp.dot(y_ref[...], w_ref[...], preferred_element_type=F32)
    o_ref[...] = _layernorm(alpha * h_ref[...] + mix, g_ref[...], b_ref[...])


def _outproj_call(h, y, w_out, g, b, layer, alpha):
    T, D = h.shape
    blocks = 4 * ROW_TILE * D * 4 + 2 * ROW_TILE * D * 2 + D * D * 2 + 2 * ROW_TILE * D * 4
    return pl.pallas_call(
        functools.partial(_outproj_kernel, alpha=alpha),
        grid=(T // ROW_TILE,),
        in_specs=[pl.BlockSpec((ROW_TILE, D), lambda i: (i, 0)),
                  pl.BlockSpec((ROW_TILE, D), lambda i: (i, 0)),
                  pl.BlockSpec((None, D, D), lambda i: (layer, 0, 0), pipeline_mode=pl.Buffered(1)),
                  pl.BlockSpec((None, 1, D), lambda i: (layer, 0, 0)),
                  pl.BlockSpec((None, 1, D), lambda i: (layer, 0, 0))],
        out_specs=pl.BlockSpec((ROW_TILE, D), lambda i: (i, 0)),
        out_shape=jax.ShapeDtypeStruct((T, D), F32),
        compiler_params=pltpu.CompilerParams(
            dimension_semantics=("parallel",),
            vmem_limit_bytes=_vmem_limit(blocks)),
        name="out_proj",
    )(h, y, w_out, g, b)


def _fill_padded(dst_ref, L, read):
    lx = L - N_META
    zeros = jnp.zeros((MARGIN, LANES), F32)
    dst_ref[0:MARGIN] = zeros
    dst_ref[MARGIN + L:2 * MARGIN + L] = zeros
    dst_ref[MARGIN:MARGIN + N_META] = read(pl.ds(lx, N_META))

    def body(i, c):
        r = pl.multiple_of(i * COPY_CHUNK, COPY_CHUNK)
        dst = pl.multiple_of(r + (MARGIN + N_META), SUBLANES)
        dst_ref[pl.ds(dst, COPY_CHUNK)] = read(pl.ds(r, COPY_CHUNK))
        return c

    lax.fori_loop(0, lx // COPY_CHUNK, body, 0)


def _for_storage_chunks(L, fn):
    lx = L - N_META
    last = lx // COPY_CHUNK - 1

    def body(i, c):
        r = pl.multiple_of(i * COPY_CHUNK, COPY_CHUNK)
        fn(pl.ds(r, COPY_CHUNK), r + N_META, COPY_CHUNK, False)
        return c

    lax.fori_loop(0, last, body, 0)
    fn(pl.ds(last * COPY_CHUNK, COPY_CHUNK), last * COPY_CHUNK + N_META, COPY_CHUNK, True)
    fn(pl.ds(lx, N_META), 0, N_META, True)


def _for_logical_chunks(L, chunk, fn):
    n_main = L // chunk

    def body(i, c):
        fn(pl.multiple_of(i * chunk, chunk), chunk)
        return c

    lax.fori_loop(0, n_main, body, 0)
    if L % chunk:
        fn(n_main * chunk, L % chunk)


def _conv_kernel(bg_ref, cg_ref, v_ref, w_ref, cb_ref, yprev_ref, o_ref, upad_ref, *, L):
    del yprev_ref
    _fill_padded(upad_ref, L, lambda rows: cg_ref[rows] * v_ref[rows])
    w = w_ref[...]
    cb = cb_ref[...]

    def emit(rows, t0, n, at_end):
        del at_end
        acc = cb
        for k in range(SHORT_CONV_WIDTH):
            off = k - SHORT_CONV_WIDTH // 2
            acc = acc + w[k:k + 1] * upad_ref[pl.ds(t0 + (MARGIN + off), n)]
        o_ref[rows] = (bg_ref[rows] * acc).astype(BF16)

    _for_storage_chunks(L, emit)


def _pool_kernel(u_ref, pw_ref, ps_ref, yprev_ref, o_ref, upad_ref, pooled_ref, *, L):
    del yprev_ref
    _fill_padded(upad_ref, L, lambda rows: u_ref[rows])
    group = pl.program_id(1)

    for gi, win in enumerate(POOL_WINDOWS):
        half = win // 2

        @pl.when(group == gi)
        def _(win=win, half=half):
            def emit(rows, t0, n, at_end):
                total = upad_ref[pl.ds(t0 + (MARGIN - half), n)]
                for j in range(1 - half, half):
                    total = total + upad_ref[pl.ds(t0 + (MARGIN + j), n)]
                if at_end:
                    t = t0 + lax.broadcasted_iota(jnp.int32, (n, LANES), 0)
                    lo = jnp.maximum(t - half, 0)
                    hi = jnp.minimum(t + (half - 1), L - 1)
                    mean = total / (hi - lo + 1).astype(F32)
                else:
                    mean = total * (1.0 / win)
                pooled_ref[rows] = (mean - upad_ref[pl.ds(t0 + MARGIN, n)]).astype(BF16)

            _for_storage_chunks(L, emit)

    mixed = jnp.dot(pooled_ref[...], pw_ref[...], preferred_element_type=F32)
    o_ref[...] = (mixed * ps_ref[...]).astype(BF16)


def _lru_kernel(x_ref, gate_ref, cw_ref, cb_ref, wg_ref, bg_ref, lam_ref, yprev_ref, o_ref,
                xpad_ref, xc_ref, a_ref, b_ref, h_ref, pre0_ref, pre1_ref, carry_ref, *, L):
    del yprev_ref
    seg = L // SCAN_SEGMENTS
    _fill_padded(xpad_ref, L, lambda rows: x_ref[rows])

    cw = cw_ref[...]
    cb = cb_ref[...]
    bg = bg_ref[...]
    neg_lam = -lam_ref[...]
    softplus = jnp.maximum(neg_lam, 0.0) + jnp.log1p(jnp.exp(-jnp.abs(neg_lam)))
    half_c2 = (-0.5 * LRU_C * LOG2_E) * softplus

    def conv(t0, n):
        for d in range(2):
            xc = cb[d:d + 1]
            for k in range(LRU_CONV_WIDTH):
                off = (k - (LRU_CONV_WIDTH - 1)) if d == 0 else ((LRU_CONV_WIDTH - 1) - k)
                xc = xc + cw[d, k:k + 1] * xpad_ref[pl.ds(t0 + (MARGIN + off), n)]
            xc_ref[d, pl.ds(t0, n)] = xc

    _for_logical_chunks(L, MIX_CHUNK, conv)

    def gate_matmuls(t0, n, pre_ref):
        for d in range(2):
            pre_ref[d, 0:n] = jnp.dot(xc_ref[d, pl.ds(t0, n)].astype(BF16), wg_ref[d],
                                      preferred_element_type=F32)

    def finish(t0, n, pre_ref):
        for d in range(2):
            pre = pre_ref[d, 0:n] + bg[d:d + 1]
            t_r = jnp.tanh(pre[:, :LANES])
            t_i = jnp.tanh(pre[:, LANES:])
            hc = half_c2[d:d + 1]
            a = jnp.exp2(hc + hc * t_r)
            gated = (0.5 * xc_ref[d, pl.ds(t0, n)]) * (1.0 + t_i)
            a_ref[d, pl.ds(t0, n)] = a
            v = 1.0 - a * a
            b_ref[d, pl.ds(t0, n)] = jnp.where(v > 0.0, v * lax.rsqrt(v), 0.0) * gated

    n_main = L // MIX_CHUNK
    assert n_main % 2 == 0
    gate_matmuls(0, MIX_CHUNK, pre0_ref)

    def gates_body(i, c):
        c0 = pl.multiple_of(2 * i * MIX_CHUNK, MIX_CHUNK)
        c1 = pl.multiple_of(c0 + MIX_CHUNK, MIX_CHUNK)
        nxt = pl.multiple_of(jnp.minimum(2 * i + 2, n_main - 1) * MIX_CHUNK, MIX_CHUNK)
        gate_matmuls(c1, MIX_CHUNK, pre1_ref)
        finish(c0, MIX_CHUNK, pre0_ref)
        gate_matmuls(nxt, MIX_CHUNK, pre0_ref)
        finish(c1, MIX_CHUNK, pre1_ref)
        return c

    lax.fori_loop(0, n_main // 2, gates_body, 0)
    if L % MIX_CHUNK:
        t0, n = n_main * MIX_CHUNK, L % MIX_CHUNK
        gate_matmuls(t0, n, pre0_ref)
        finish(t0, n, pre0_ref)

    nv = SCAN_SEGMENTS // SUBLANES

    def seg_rows(j, t):
        return pl.ds(j * SUBLANES * seg + t, SUBLANES, stride=seg)

    def ends_body(t, c):
        out = []
        for d in range(2):
            tt = t if d == 0 else seg - 1 - t
            for j in range(nv):
                h, p = c[2 * (d * nv + j)], c[2 * (d * nv + j) + 1]
                rows = seg_rows(j, tt)
                a = a_ref[d, rows]
                out += [a * h + b_ref[d, rows], p * a]
        return tuple(out)

    zero = jnp.zeros((SUBLANES, LANES), F32)
    one = jnp.ones((SUBLANES, LANES), F32)
    ends = lax.fori_loop(0, seg, ends_body, (zero, one) * (2 * nv), unroll=4)

    S = SCAN_SEGMENTS
    for d in range(2):
        for j in range(nv):
            carry_ref[d, pl.ds(j * SUBLANES, SUBLANES)] = ends[2 * (d * nv + j)]
            carry_ref[d, pl.ds(S + j * SUBLANES, SUBLANES)] = ends[2 * (d * nv + j) + 1]
        state = jnp.zeros((1, LANES), F32)
        for s in (range(S) if d == 0 else reversed(range(S))):
            carry_ref[d, 2 * S + s:2 * S + s + 1] = state
            state = carry_ref[d, S + s:S + s + 1] * state + carry_ref[d, s:s + 1]
    cin = tuple(carry_ref[d, pl.ds(2 * S + j * SUBLANES, SUBLANES)] for d in range(2) for j in range(nv))

    def scan_body(t, c):
        out = []
        for d in range(2):
            tt = t if d == 0 else seg - 1 - t
            for j in range(nv):
                rows = seg_rows(j, tt)
                h = a_ref[d, rows] * c[d * nv + j] + b_ref[d, rows]
                h_ref[d, rows] = h
                out.append(h)
        return tuple(out)

    lax.fori_loop(0, seg, scan_body, cin, unroll=4)

    def emit(rows, t0, n, at_end):
        del at_end
        g = gate_ref[rows]
        t = jnp.tanh(g * (GELU_C0 + (GELU_C0 * GELU_C1) * (g * g)))
        hsum = h_ref[0, pl.ds(t0, n)] + h_ref[1, pl.ds(t0, n)]
        o_ref[rows] = (((0.5 * g) * (1.0 + t)) * hsum).astype(BF16)

    _for_storage_chunks(L, emit)


def _mixer_params(nseq, nblk, scratch_bytes):
    return dict(
        grid=(nseq, nblk),
        compiler_params=pltpu.CompilerParams(
            dimension_semantics=("parallel", "arbitrary"),
            vmem_limit_bytes=_vmem_limit(scratch_bytes)),
    )


def _conv_call(proj, y, conv_w, conv_b, layer, L, nseq, blk0, dconv):
    nb = dconv // LANES
    T, D = y.shape
    seq = lambda base: pl.BlockSpec((L, LANES), lambda b, j: (b + blk0, base + j))
    return pl.pallas_call(
        functools.partial(_conv_kernel, L=L),
        in_specs=[seq(0), seq(nb), seq(2 * nb),
                  pl.BlockSpec((None, SHORT_CONV_WIDTH, LANES), lambda b, j: (layer, 0, j)),
                  pl.BlockSpec((None, 1, LANES), lambda b, j: (layer, 0, j)),
                  pl.BlockSpec(memory_space=pl.ANY)],
        out_specs=pl.BlockSpec((L, LANES), lambda b, j: (b + blk0, j)),
        out_shape=jax.ShapeDtypeStruct((T, D), BF16),
        scratch_shapes=[pltpu.VMEM((L + 2 * MARGIN, LANES), F32)],
        input_output_aliases={5: 0},
        name="mix_conv",
        **_mixer_params(nseq, nb, 9 * L * LANES * 4),
    )(proj, proj, proj, conv_w, conv_b, y)


def _lru_call(proj, y, cw, cb, wg, bg, lam, layer, L, nseq, blk0, col0, out0, nheads):
    T, D = y.shape
    c0 = col0 // LANES
    return pl.pallas_call(
        functools.partial(_lru_kernel, L=L),
        in_specs=[pl.BlockSpec((L, LANES), lambda b, j: (b + blk0, c0 + j)),
                  pl.BlockSpec((L, LANES), lambda b, j: (b + blk0, c0 + nheads + j)),
                  pl.BlockSpec((None, 2, LRU_CONV_WIDTH, LANES), lambda b, j: (layer, 0, 0, j)),
                  pl.BlockSpec((None, 2, LANES), lambda b, j: (layer, 0, j)),
                  pl.BlockSpec((None, None, 2, LANES, 2 * LANES), lambda b, j: (layer, j, 0, 0, 0)),
                  pl.BlockSpec((None, None, 2, 2 * LANES), lambda b, j: (layer, j, 0, 0)),
                  pl.BlockSpec((None, 2, LANES), lambda b, j: (layer, 0, j)),
                  pl.BlockSpec(memory_space=pl.ANY)],
        out_specs=pl.BlockSpec((L, LANES), lambda b, j: (b + blk0, out0 // LANES + j)),
        out_shape=jax.ShapeDtypeStruct((T, D), BF16),
        scratch_shapes=[pltpu.VMEM((L + 2 * MARGIN, LANES), F32)]
        + [pltpu.VMEM((2, L, LANES), F32)] * 4
        + [pltpu.VMEM((2, MIX_CHUNK, 2 * LANES), F32)] * 2
        + [pltpu.VMEM((2, 3 * SCAN_SEGMENTS, LANES), F32)],
        input_output_aliases={7: 0},
        name="mix_lru",
        **_mixer_params(nseq, nheads, 15 * L * LANES * 4),
    )(proj, proj, cw, cb, wg, bg, lam, y)


def _pool_call(proj, y, pool_w, pool_scale, layer, L, nseq, blk0, col0, out0):
    T, D = y.shape
    ng = len(POOL_WINDOWS)
    return pl.pallas_call(
        functools.partial(_pool_kernel, L=L),
        in_specs=[pl.BlockSpec((L, LANES), lambda b, j: (b + blk0, col0 // LANES + j)),
                  pl.BlockSpec((None, None, LANES, LANES), lambda b, j: (layer, j, 0, 0)),
                  pl.BlockSpec((None, 1, LANES), lambda b, j: (layer, 0, j)),
                  pl.BlockSpec(memory_space=pl.ANY)],
        out_specs=pl.BlockSpec((L, LANES), lambda b, j: (b + blk0, out0 // LANES + j)),
        out_shape=jax.ShapeDtypeStruct((T, D), BF16),
        scratch_shapes=[pltpu.VMEM((L + 2 * MARGIN, LANES), F32), pltpu.VMEM((L, LANES), BF16)],
        input_output_aliases={3: 0},
        name="mix_pool",
        **_mixer_params(nseq, ng, 5 * L * LANES * 4),
    )(proj, pool_w, pool_scale, y)


def kernel(x_prompt, x_sample, meta_tokens, ln_in_g, ln_in_b, ffn1_w_gate, ffn1_w_up, ffn1_w_down, ln1_g, ln1_b, w_in, conv_w, conv_b, lru_conv_w, lru_conv_b, lru_w_a, lru_b_a, lru_w_x, lru_b_x, lru_lambda, pool_w, pool_scale, w_out, ln2_g, ln2_b, ffn2_w_gate, ffn2_w_up, ffn2_w_down, ln3_g, ln3_b):
    depth = ffn1_w_gate.shape[0]
    alpha = float((2 * depth) ** 0.25)
    bp, sp, D = x_prompt.shape
    bs, ss, _ = x_sample.shape
    lp, ls = sp + N_META, ss + N_META
    dconv = conv_w.shape[-1]
    dlru = lru_lambda.shape[-1]
    dpool = pool_scale.shape[-1]
    nheads = lru_w_a.shape[2]
    assert D == dconv + dlru + dpool and dlru // nheads == LANES and dpool // len(POOL_WINDOWS) == LANES
    assert dconv % LANES == 0 and sp % ROW_TILE == 0 and ss % ROW_TILE == 0
    assert lp % SCAN_SEGMENTS == 0 and ls % SCAN_SEGMENTS == 0
    assert max(POOL_WINDOWS) // 2 <= min(N_META, MARGIN)

    ts = bs * ls
    blk0_p = -(-ts // lp)
    off_p = blk0_p * lp
    t_real = off_p + bp * lp
    t_pad = -(-t_real // ROW_TILE) * ROW_TILE
    groups = ((x_sample, ls, bs, 0, ss), (x_prompt, lp, bp, blk0_p, sp))

    bf = lambda w: w.astype(BF16)
    row3 = lambda p: p.reshape(depth, 1, -1)
    f1 = (bf(ffn1_w_gate), bf(ffn1_w_up), bf(ffn1_w_down), row3(ln1_g), row3(ln1_b))
    f2 = (bf(ffn2_w_gate), bf(ffn2_w_up), bf(ffn2_w_down), row3(ln3_g), row3(ln3_b))
    w_in_b, w_out_b = bf(w_in), bf(w_out)
    gate_w = bf(0.5 * jnp.concatenate([lru_w_a, lru_w_x], axis=-1)).transpose(0, 2, 1, 3, 4)
    gate_b = 0.5 * jnp.concatenate([lru_b_a.reshape(depth, 2, nheads, LANES),
                                    lru_b_x.reshape(depth, 2, nheads, LANES)], axis=-1).transpose(0, 2, 1, 3)
    pool_w_b = bf(pool_w)
    conv_b3, pool_s3 = row3(conv_b), row3(pool_scale)
    ln2 = (row3(ln2_g), row3(ln2_b))
    lru_col, pool_col = 3 * dconv, 3 * dconv + 2 * dlru

    h = jnp.zeros((t_pad, D), F32)
    y = jnp.zeros((t_pad, D), BF16)
    for x, L, nseq, blk0, S in groups:
        h = _ln_in_call(x, h, ln_in_g, ln_in_b, L, blk0, ROW_TILE, 0)
        meta = jnp.broadcast_to(meta_tokens[None], (nseq, N_META, D))
        h = _ln_in_call(meta, h, ln_in_g, ln_in_b, L, blk0, N_META, S)

    outs = []
    for layer in range(depth):
        h = _ffn_call(h, *f1, layer, alpha)
        proj = _inproj_call(h, w_in_b, layer)
        for _, L, nseq, blk0, _ in groups:
            y = _conv_call(proj, y, conv_w, conv_b3, layer, L, nseq, blk0, dconv)
            y = _lru_call(proj, y, lru_conv_w, lru_conv_b, gate_w, gate_b, lru_lambda,
                          layer, L, nseq, blk0, lru_col, dconv, nheads)
            y = _pool_call(proj, y, pool_w_b, pool_s3, layer, L, nseq, blk0, pool_col, dconv + dlru)
        h = _outproj_call(h, y, w_out_b, *ln2, layer, alpha)
        if layer + 1 < depth:
            h = _ffn_call(h, *f2, layer, alpha)
        else:
            outs = [_ffn_call(h, *f2, layer, alpha, seq=(L, nseq, blk0, S)).reshape(nseq, S, D)
                    for _, L, nseq, blk0, S in groups]

    y_sample, y_prompt = outs
    return (y_prompt, y_sample)
```

```python
import functools

import jax
import jax.numpy as jnp
from jax import lax
from jax.experimental import pallas as pl
from jax.experimental.pallas import tpu as pltpu

F32 = jnp.float32
BF16 = jnp.bfloat16

N_META = 16
POOL_WINDOWS = (2, 4, 8, 16)
SHORT_CONV_WIDTH = 3
LRU_CONV_WIDTH = 4
LRU_C = 8.0
LN_EPS = 1e-5
LOG2_E = 1.4426950408889634
GELU_C0 = 0.7978845608028654
GELU_C1 = 0.044715

LANES = 128
SUBLANES = 8
BF16_SUBLANES = 16
VMEM_BYTES_V7X = 64 * 1024 * 1024

ROW_TILE = 512
FF_TILE = 512
PROJ_TILE = 1024
MIX_CHUNK = 256
COPY_CHUNK = 256
MARGIN = SUBLANES
SCAN_SEGMENTS = 2 * SUBLANES


def _vmem_limit(block_bytes):
    return int(min(block_bytes + block_bytes // 2 + (8 << 20), VMEM_BYTES_V7X - (6 << 20)))


def _layernorm(z, g, b):
    mu = jnp.mean(z, axis=-1, keepdims=True)
    d = z - mu
    var = jnp.mean(d * d, axis=-1, keepdims=True)
    return d * lax.rsqrt(var + LN_EPS) * g + b


def _ln_kernel(x_ref, g_ref, b_ref, hprev_ref, o_ref):
    del hprev_ref
    o_ref[...] = _layernorm(x_ref[...], g_ref[...], b_ref[...])


def _ln_in_call(x, h, g, b, L, blk0, rows_per_step, row0):
    nseq, S, D = x.shape
    T = h.shape[0]
    nj = S // rows_per_step
    return pl.pallas_call(
        _ln_kernel,
        grid=(nseq, nj),
        in_specs=[pl.BlockSpec((None, rows_per_step, D), lambda s, j: (s, j, 0)),
                  pl.BlockSpec((1, D), lambda s, j: (0, 0)),
                  pl.BlockSpec((1, D), lambda s, j: (0, 0)),
                  pl.BlockSpec(memory_space=pl.ANY)],
        out_specs=pl.BlockSpec((pl.Element(rows_per_step), pl.Element(D)),
                               lambda s, j: (pl.multiple_of((blk0 + s) * L + row0 + j * rows_per_step,
                                                            SUBLANES), 0)),
        out_shape=jax.ShapeDtypeStruct((T, D), F32),
        input_output_aliases={3: 0},
        compiler_params=pltpu.CompilerParams(
            dimension_semantics=("arbitrary", "arbitrary"),
            vmem_limit_bytes=_vmem_limit(4 * rows_per_step * D * 4)),
        name="ln_in",
    )(x, g.reshape(1, D), b.reshape(1, D), h)


def _ffn_kernel(x_ref, wg_ref, wu_ref, wd_ref, g_ref, b_ref, o_ref, xb_ref, acc_ref, z_ref, *,
                nk, n_tiles, alpha):
    i = pl.program_id(0)
    k = pl.program_id(1)
    real = i < n_tiles
    middle = (k > 0) & (k < nk - 1)
    ln_steps = 1 << ((nk - 2).bit_length() - 1)
    ln_rows = ROW_TILE // ln_steps

    def down(xb):
        gate = jnp.dot(xb, wg_ref[...], preferred_element_type=F32)
        up = jnp.dot(xb, wu_ref[...], preferred_element_type=F32)
        hidden = (gate * jax.nn.sigmoid(gate) * up).astype(BF16)
        return jnp.dot(hidden, wd_ref[...], preferred_element_type=F32)

    def normalise_slice():
        s = jnp.minimum(k - 1, ln_steps - 1)
        rows = pl.ds(pl.multiple_of(s * ln_rows, ln_rows), ln_rows)
        o_ref[rows] = _layernorm(z_ref[rows], g_ref[...], b_ref[...])

    @pl.when((i == 0) & (k == 0))
    def _():
        z_ref[...] = jnp.zeros_like(z_ref)

    @pl.when(real & (k == 0))
    def _():
        xb = x_ref[...].astype(BF16)
        xb_ref[...] = xb
        acc_ref[...] = down(xb)

    @pl.when(real & middle)
    def _():
        normalise_slice()
        acc_ref[...] += down(xb_ref[...])

    @pl.when(real & (k == nk - 1))
    def _():
        z_ref[...] = alpha * x_ref[...] + 0.5 * (acc_ref[...] + down(xb_ref[...]))

    @pl.when(jnp.logical_not(real) & middle)
    def _():
        normalise_slice()


def _ffn_call(h, wg, wu, wd, g, b, layer, alpha, seq=None):
    T, D = h.shape
    F = wg.shape[-1]
    nk = F // FF_TILE
    assert nk >= 3
    if seq is None:
        n_tiles, out_rows = T // ROW_TILE, T
        x_spec = pl.BlockSpec((ROW_TILE, D), lambda i, k: (jnp.minimum(i, n_tiles - 1), 0))
    else:
        L, nseq, blk0, S = seq
        nj = S // ROW_TILE
        n_tiles, out_rows = nseq * nj, nseq * S

        def x_rows(i, k):
            t = jnp.minimum(i, n_tiles - 1)
            return (pl.multiple_of((blk0 + t // nj) * L + (t % nj) * ROW_TILE, SUBLANES), 0)

        x_spec = pl.BlockSpec((pl.Element(ROW_TILE), pl.Element(D)), x_rows)
    kk = lambda i, k: jnp.where(i < n_tiles, k, nk - 1)
    blocks = (4 * ROW_TILE * D * 4 + 6 * D * FF_TILE * 2 + ROW_TILE * D * (2 + 4 + 4)
              + 4 * ROW_TILE * FF_TILE * 4)
    return pl.pallas_call(
        functools.partial(_ffn_kernel, nk=nk, n_tiles=n_tiles, alpha=alpha),
        grid=(n_tiles + 1, nk),
        in_specs=[x_spec,
                  pl.BlockSpec((None, D, FF_TILE), lambda i, k: (layer, 0, kk(i, k))),
                  pl.BlockSpec((None, D, FF_TILE), lambda i, k: (layer, 0, kk(i, k))),
                  pl.BlockSpec((None, FF_TILE, D), lambda i, k: (layer, kk(i, k), 0)),
                  pl.BlockSpec((None, 1, D), lambda i, k: (layer, 0, 0)),
                  pl.BlockSpec((None, 1, D), lambda i, k: (layer, 0, 0))],
        out_specs=pl.BlockSpec((ROW_TILE, D), lambda i, k: (jnp.maximum(i - 1, 0), 0)),
        out_shape=jax.ShapeDtypeStruct((out_rows, D), F32),
        scratch_shapes=[pltpu.VMEM((ROW_TILE, D), BF16), pltpu.VMEM((ROW_TILE, D), F32),
                        pltpu.VMEM((ROW_TILE, D), F32)],
        compiler_params=pltpu.CompilerParams(
            dimension_semantics=("arbitrary", "arbitrary"),
            vmem_limit_bytes=_vmem_limit(blocks)),
        name="ffn",
    )(h, wg, wu, wd, g, b)


def _inproj_kernel(x_ref, w_ref, o_ref):
    xb = x_ref[...].astype(BF16)
    for n in range(o_ref.shape[-1] // PROJ_TILE):
        cols = slice(n * PROJ_TILE, (n + 1) * PROJ_TILE)
        o_ref[:, cols] = jnp.dot(xb, w_ref[:, cols], preferred_element_type=F32)


def _inproj_call(h, w_in, layer):
    T, D = h.shape
    P = w_in.shape[-1]
    blocks = 2 * ROW_TILE * D * 4 + D * P * 2 + 2 * ROW_TILE * P * 4 + ROW_TILE * D * 2
    return pl.pallas_call(
        _inproj_kernel,
        grid=(T // ROW_TILE,),
        in_specs=[pl.BlockSpec((ROW_TILE, D), lambda i: (i, 0)),
                  pl.BlockSpec((None, D, P), lambda i: (layer, 0, 0), pipeline_mode=pl.Buffered(1))],
        out_specs=pl.BlockSpec((ROW_TILE, P), lambda i: (i, 0)),
        out_shape=jax.ShapeDtypeStruct((T, P), F32),
        compiler_params=pltpu.CompilerParams(
            dimension_semantics=("parallel",),
            vmem_limit_bytes=_vmem_limit(blocks)),
        name="in_proj",
    )(h, w_in)


def _outproj_kernel(h_ref, y_ref, w_ref, g_ref, b_ref, o_ref, *, alpha):
    mix = jnp.dot(y_ref[...], w_ref[...], preferred_element_type=F32)
    o_ref[...] = _layernorm(alpha * h_ref[...] + mix, g_ref[...], b_ref[...])


def _outproj_call(h, y, w_out, g, b, layer, alpha):
    T, D = h.shape
    blocks = 4 * ROW_TILE * D * 4 + 2 * ROW_TILE * D * 2 + D * D * 2 + 2 * ROW_TILE * D * 4
    return pl.pallas_call(
        functools.partial(_outproj_kernel, alpha=alpha),
        grid=(T // ROW_TILE,),
        in_specs=[pl.BlockSpec((ROW_TILE, D), lambda i: (i, 0)),
                  pl.BlockSpec((ROW_TILE, D), lambda i: (i, 0)),
                  pl.BlockSpec((None, D, D), lambda i: (layer, 0, 0), pipeline_mode=pl.Buffered(1)),
                  pl.BlockSpec((None, 1, D), lambda i: (layer, 0, 0)),
                  pl.BlockSpec((None, 1, D), lambda i: (layer, 0, 0))],
        out_specs=pl.BlockSpec((ROW_TILE, D), lambda i: (i, 0)),
        out_shape=jax.ShapeDtypeStruct((T, D), F32),
        compiler_params=pltpu.CompilerParams(
            dimension_semantics=("parallel",),
            vmem_limit_bytes=_vmem_limit(blocks)),
        name="out_proj",
    )(h, y, w_out, g, b)


def _fill_padded(dst_ref, L, read):
    lx = L - N_META
    zeros = jnp.zeros((MARGIN, LANES), F32)
    dst_ref[0:MARGIN] = zeros
    dst_ref[MARGIN + L:2 * MARGIN + L] = zeros
    dst_ref[MARGIN:MARGIN + N_META] = read(pl.ds(lx, N_META))

    def body(i, c):
        r = pl.multiple_of(i * COPY_CHUNK, COPY_CHUNK)
        dst = pl.multiple_of(r + (MARGIN + N_META), SUBLANES)
        dst_ref[pl.ds(dst, COPY_CHUNK)] = read(pl.ds(r, COPY_CHUNK))
        return c

    lax.fori_loop(0, lx // COPY_CHUNK, body, 0)


def _for_storage_chunks(L, fn):
    lx = L - N_META
    last = lx // COPY_CHUNK - 1

    def body(i, c):
        r = pl.multiple_of(i * COPY_CHUNK, COPY_CHUNK)
        fn(pl.ds(r, COPY_CHUNK), r + N_META, COPY_CHUNK, False)
        return c

    lax.fori_loop(0, last, body, 0)
    fn(pl.ds(last * COPY_CHUNK, COPY_CHUNK), last * COPY_CHUNK + N_META, COPY_CHUNK, True)
    fn(pl.ds(lx, N_META), 0, N_META, True)


def _for_logical_chunks(L, chunk, fn):
    n_main = L // chunk

    def body(i, c):
        fn(pl.multiple_of(i * chunk, chunk), chunk)
        return c

    lax.fori_loop(0, n_main, body, 0)
    if L % chunk:
        fn(n_main * chunk, L % chunk)


def _conv_kernel(bg_ref, cg_ref, v_ref, w_ref, cb_ref, yprev_ref, o_ref, upad_ref, *, L):
    del yprev_ref
    _fill_padded(upad_ref, L, lambda rows: cg_ref[rows] * v_ref[rows])
    w = w_ref[...]
    cb = cb_ref[...]

    def emit(rows, t0, n, at_end):
        del at_end
        acc = cb
        for k in range(SHORT_CONV_WIDTH):
            off = k - SHORT_CONV_WIDTH // 2
            acc = acc + w[k:k + 1] * upad_ref[pl.ds(t0 + (MARGIN + off), n)]
        o_ref[rows] = (bg_ref[rows] * acc).astype(BF16)

    _for_storage_chunks(L, emit)


def _pool_kernel(u_ref, pw_ref, ps_ref, yprev_ref, o_ref, upad_ref, pooled_ref, *, L):
    del yprev_ref
    _fill_padded(upad_ref, L, lambda rows: u_ref[rows])
    group = pl.program_id(1)

    for gi, win in enumerate(POOL_WINDOWS):
        half = win // 2

        @pl.when(group == gi)
        def _(win=win, half=half):
            def emit(rows, t0, n, at_end):
                total = upad_ref[pl.ds(t0 + (MARGIN - half), n)]
                for j in range(1 - half, half):
                    total = total + upad_ref[pl.ds(t0 + (MARGIN + j), n)]
                if at_end:
                    t = t0 + lax.broadcasted_iota(jnp.int32, (n, LANES), 0)
                    lo = jnp.maximum(t - half, 0)
                    hi = jnp.minimum(t + (half - 1), L - 1)
                    mean = total / (hi - lo + 1).astype(F32)
                else:
                    mean = total * (1.0 / win)
                pooled_ref[rows] = (mean - upad_ref[pl.ds(t0 + MARGIN, n)]).astype(BF16)

            _for_storage_chunks(L, emit)

    mixed = jnp.dot(pooled_ref[...], pw_ref[...], preferred_element_type=F32)
    o_ref[...] = (mixed * ps_ref[...]).astype(BF16)


def _lru_kernel(x_ref, gate_ref, cw_ref, cb_ref, wg_ref, bg_ref, lam_ref, yprev_ref, o_ref,
                xpad_ref, xc_ref, a_ref, b_ref, h_ref, pre0_ref, pre1_ref, carry_ref, *, L):
    del yprev_ref
    seg = L // SCAN_SEGMENTS
    _fill_padded(xpad_ref, L, lambda rows: x_ref[rows])

    cw = cw_ref[...]
    cb = cb_ref[...]
    bg = bg_ref[...]
    neg_lam = -lam_ref[...]
    softplus = jnp.maximum(neg_lam, 0.0) + jnp.log1p(jnp.exp(-jnp.abs(neg_lam)))
    half_c2 = (-0.5 * LRU_C * LOG2_E) * softplus

    def conv(t0, n):
        for d in range(2):
            xc = cb[d:d + 1]
            for k in range(LRU_CONV_WIDTH):
                off = (k - (LRU_CONV_WIDTH - 1)) if d == 0 else ((LRU_CONV_WIDTH - 1) - k)
                xc = xc + cw[d, k:k + 1] * xpad_ref[pl.ds(t0 + (MARGIN + off), n)]
            xc_ref[d, pl.ds(t0, n)] = xc

    _for_logical_chunks(L, MIX_CHUNK, conv)

    def gate_matmuls(t0, n, pre_ref):
        for d in range(2):
            pre_ref[d, 0:n] = jnp.dot(xc_ref[d, pl.ds(t0, n)].astype(BF16), wg_ref[d],
                                      preferred_element_type=F32)

    def finish(t0, n, pre_ref):
        for d in range(2):
            pre = pre_ref[d, 0:n] + bg[d:d + 1]
            t_r = jnp.tanh(pre[:, :LANES])
            t_i = jnp.tanh(pre[:, LANES:])
            hc = half_c2[d:d + 1]
            a = jnp.exp2(hc + hc * t_r)
            gated = (0.5 * xc_ref[d, pl.ds(t0, n)]) * (1.0 + t_i)
            a_ref[d, pl.ds(t0, n)] = a
            v = 1.0 - a * a
            b_ref[d, pl.ds(t0, n)] = jnp.where(v > 0.0, v * lax.rsqrt(v), 0.0) * gated

    n_main = L // MIX_CHUNK
    assert n_main % 2 == 0
    gate_matmuls(0, MIX_CHUNK, pre0_ref)

    def gates_body(i, c):
        c0 = pl.multiple_of(2 * i * MIX_CHUNK, MIX_CHUNK)
        c1 = pl.multiple_of(c0 + MIX_CHUNK, MIX_CHUNK)
        nxt = pl.multiple_of(jnp.minimum(2 * i + 2, n_main - 1) * MIX_CHUNK, MIX_CHUNK)
        gate_matmuls(c1, MIX_CHUNK, pre1_ref)
        finish(c0, MIX_CHUNK, pre0_ref)
        gate_matmuls(nxt, MIX_CHUNK, pre0_ref)
        finish(c1, MIX_CHUNK, pre1_ref)
        return c

    lax.fori_loop(0, n_main // 2, gates_body, 0)
    if L % MIX_CHUNK:
        t0, n = n_main * MIX_CHUNK, L % MIX_CHUNK
        gate_matmuls(t0, n, pre0_ref)
        finish(t0, n, pre0_ref)

    nv = SCAN_SEGMENTS // SUBLANES

    def seg_rows(j, t):
        return pl.ds(j * SUBLANES * seg + t, SUBLANES, stride=seg)

    def ends_body(t, c):
        out = []
        for d in range(2):
            tt = t if d == 0 else seg - 1 - t
            for j in range(nv):
                h, p = c[2 * (d * nv + j)], c[2 * (d * nv + j) + 1]
                rows = seg_rows(j, tt)
                a = a_ref[d, rows]
                out += [a * h + b_ref[d, rows], p * a]
        return tuple(out)

    zero = jnp.zeros((SUBLANES, LANES), F32)
    one = jnp.ones((SUBLANES, LANES), F32)
    ends = lax.fori_loop(0, seg, ends_body, (zero, one) * (2 * nv), unroll=4)

    S = SCAN_SEGMENTS
    for d in range(2):
        for j in range(nv):
            carry_ref[d, pl.ds(j * SUBLANES, SUBLANES)] = ends[2 * (d * nv + j)]
            carry_ref[d, pl.ds(S + j * SUBLANES, SUBLANES)] = ends[2 * (d * nv + j) + 1]
        state = jnp.zeros((1, LANES), F32)
        for s in (range(S) if d == 0 else reversed(range(S))):
            carry_ref[d, 2 * S + s:2 * S + s + 1] = state
            state = carry_ref[d, S + s:S + s + 1] * state + carry_ref[d, s:s + 1]
    cin = tuple(carry_ref[d, pl.ds(2 * S + j * SUBLANES, SUBLANES)] for d in range(2) for j in range(nv))

    def scan_body(t, c):
        out = []
        for d in range(2):
            tt = t if d == 0 else seg - 1 - t
            for j in range(nv):
                rows = seg_rows(j, tt)
                h = a_ref[d, rows] * c[d * nv + j] + b_ref[d, rows]
                h_ref[d, rows] = h
                out.append(h)
        return tuple(out)

    lax.fori_loop(0, seg, scan_body, cin, unroll=4)

    def emit(rows, t0, n, at_end):
        del at_end
        g = gate_ref[rows]
        t = jnp.tanh(g * (GELU_C0 + (GELU_C0 * GELU_C1) * (g * g)))
        hsum = h_ref[0, pl.ds(t0, n)] + h_ref[1, pl.ds(t0, n)]
        o_ref[rows] = (((0.5 * g) * (1.0 + t)) * hsum).astype(BF16)

    _for_storage_chunks(L, emit)


def _mixer_params(nseq, nblk, scratch_bytes):
    return dict(
        grid=(nseq, nblk),
        compiler_params=pltpu.CompilerParams(
            dimension_semantics=("parallel", "arbitrary"),
            vmem_limit_bytes=_vmem_limit(scratch_bytes)),
    )


def _conv_call(proj, y, conv_w, conv_b, layer, L, nseq, blk0, dconv):
    nb = dconv // LANES
    T, D = y.shape
    seq = lambda base: pl.BlockSpec((L, LANES), lambda b, j: (b + blk0, base + j))
    return pl.pallas_call(
        functools.partial(_conv_kernel, L=L),
        in_specs=[seq(0), seq(nb), seq(2 * nb),
                  pl.BlockSpec((None, SHORT_CONV_WIDTH, LANES), lambda b, j: (layer, 0, j)),
                  pl.BlockSpec((None, 1, LANES), lambda b, j: (layer, 0, j)),
                  pl.BlockSpec(memory_space=pl.ANY)],
        out_specs=pl.BlockSpec((L, LANES), lambda b, j: (b + blk0, j)),
        out_shape=jax.ShapeDtypeStruct((T, D), BF16),
        scratch_shapes=[pltpu.VMEM((L + 2 * MARGIN, LANES), F32)],
        input_output_aliases={5: 0},
        name="mix_conv",
        **_mixer_params(nseq, nb, 9 * L * LANES * 4),
    )(proj, proj, proj, conv_w, conv_b, y)


def _lru_call(proj, y, cw, cb, wg, bg, lam, layer, L, nseq, blk0, col0, out0, nheads):
    T, D = y.shape
    c0 = col0 // LANES
    return pl.pallas_call(
        functools.partial(_lru_kernel, L=L),
        in_specs=[pl.BlockSpec((L, LANES), lambda b, j: (b + blk0, c0 + j)),
                  pl.BlockSpec((L, LANES), lambda b, j: (b + blk0, c0 + nheads + j)),
                  pl.BlockSpec((None, 2, LRU_CONV_WIDTH, LANES), lambda b, j: (layer, 0, 0, j)),
                  pl.BlockSpec((None, 2, LANES), lambda b, j: (layer, 0, j)),
                  pl.BlockSpec((None, None, 2, LANES, 2 * LANES), lambda b, j: (layer, j, 0, 0, 0)),
                  pl.BlockSpec((None, None, 2, 2 * LANES), lambda b, j: (layer, j, 0, 0)),
                  pl.BlockSpec((None, 2, LANES), lambda b, j: (layer, 0, j)),
                  pl.BlockSpec(memory_space=pl.ANY)],
        out_specs=pl.BlockSpec((L, LANES), lambda b, j: (b + blk0, out0 // LANES + j)),
        out_shape=jax.ShapeDtypeStruct((T, D), BF16),
        scratch_shapes=[pltpu.VMEM((L + 2 * MARGIN, LANES), F32)]
        + [pltpu.VMEM((2, L, LANES), F32)] * 4
        + [pltpu.VMEM((2, MIX_CHUNK, 2 * LANES), F32)] * 2
        + [pltpu.VMEM((2, 3 * SCAN_SEGMENTS, LANES), F32)],
        input_output_aliases={7: 0},
        name="mix_lru",
        **_mixer_params(nseq, nheads, 15 * L * LANES * 4),
    )(proj, proj, cw, cb, wg, bg, lam, y)


def _pool_call(proj, y, pool_w, pool_scale, layer, L, nseq, blk0, col0, out0):
    T, D = y.shape
    ng = len(POOL_WINDOWS)
    return pl.pallas_call(
        functools.partial(_pool_kernel, L=L),
        in_specs=[pl.BlockSpec((L, LANES), lambda b, j: (b + blk0, col0 // LANES + j)),
                  pl.BlockSpec((None, None, LANES, LANES), lambda b, j: (layer, j, 0, 0)),
                  pl.BlockSpec((None, 1, LANES), lambda b, j: (layer, 0, j)),
                  pl.BlockSpec(memory_space=pl.ANY)],
        out_specs=pl.BlockSpec((L, LANES), lambda b, j: (b + blk0, out0 // LANES + j)),
        out_shape=jax.ShapeDtypeStruct((T, D), BF16),
        scratch_shapes=[pltpu.VMEM((L + 2 * MARGIN, LANES), F32), pltpu.VMEM((L, LANES), BF16)],
        input_output_aliases={3: 0},
        name="mix_pool",
        **_mixer_params(nseq, ng, 5 * L * LANES * 4),
    )(proj, pool_w, pool_scale, y)


def kernel(x_prompt, x_sample, meta_tokens, ln_in_g, ln_in_b, ffn1_w_gate, ffn1_w_up, ffn1_w_down, ln1_g, ln1_b, w_in, conv_w, conv_b, lru_conv_w, lru_conv_b, lru_w_a, lru_b_a, lru_w_x, lru_b_x, lru_lambda, pool_w, pool_scale, w_out, ln2_g, ln2_b, ffn2_w_gate, ffn2_w_up, ffn2_w_down, ln3_g, ln3_b):
    depth = ffn1_w_gate.shape[0]
    alpha = float((2 * depth) ** 0.25)
    bp, sp, D = x_prompt.shape
    bs, ss, _ = x_sample.shape
    lp, ls = sp + N_META, ss + N_META
    dconv = conv_w.shape[-1]
    dlru = lru_lambda.shape[-1]
    dpool = pool_scale.shape[-1]
    nheads = lru_w_a.shape[2]
    assert D == dconv + dlru + dpool and dlru // nheads == LANES and dpool // len(POOL_WINDOWS) == LANES
    assert dconv % LANES == 0 and sp % ROW_TILE == 0 and ss % ROW_TILE == 0
    assert lp % SCAN_SEGMENTS == 0 and ls % SCAN_SEGMENTS == 0
    assert max(POOL_WINDOWS) // 2 <= min(N_META, MARGIN)

    ts = bs * ls
    blk0_p = -(-ts // lp)
    off_p = blk0_p * lp
    t_real = off_p + bp * lp
    t_pad = -(-t_real // ROW_TILE) * ROW_TILE
    groups = ((x_sample, ls, bs, 0, ss), (x_prompt, lp, bp, blk0_p, sp))

    bf = lambda w: w.astype(BF16)
    row3 = lambda p: p.reshape(depth, 1, -1)
    f1 = (bf(ffn1_w_gate), bf(ffn1_w_up), bf(ffn1_w_down), row3(ln1_g), row3(ln1_b))
    f2 = (bf(ffn2_w_gate), bf(ffn2_w_up), bf(ffn2_w_down), row3(ln3_g), row3(ln3_b))
    w_in_b, w_out_b = bf(w_in), bf(w_out)
    gate_w = bf(0.5 * jnp.concatenate([lru_w_a, lru_w_x], axis=-1)).transpose(0, 2, 1, 3, 4)
    gate_b = 0.5 * jnp.concatenate([lru_b_a.reshape(depth, 2, nheads, LANES),
                                    lru_b_x.reshape(depth, 2, nheads, LANES)], axis=-1).transpose(0, 2, 1, 3)
    pool_w_b = bf(pool_w)
    conv_b3, pool_s3 = row3(conv_b), row3(pool_scale)
    ln2 = (row3(ln2_g), row3(ln2_b))
    lru_col, pool_col = 3 * dconv, 3 * dconv + 2 * dlru

    h = jnp.zeros((t_pad, D), F32)
    y = jnp.zeros((t_pad, D), BF16)
    for x, L, nseq, blk0, S in groups:
        h = _ln_in_call(x, h, ln_in_g, ln_in_b, L, blk0, ROW_TILE, 0)
        meta = jnp.broadcast_to(meta_tokens[None], (nseq, N_META, D))
        h = _ln_in_call(meta, h, ln_in_g, ln_in_b, L, blk0, N_META, S)

    outs = []
    for layer in range(depth):
        h = _ffn_call(h, *f1, layer, alpha)
        proj = _inproj_call(h, w_in_b, layer)
        for _, L, nseq, blk0, _ in groups:
            y = _conv_call(proj, y, conv_w, conv_b3, layer, L, nseq, blk0, dconv)
            y = _lru_call(proj, y, lru_conv_w, lru_conv_b, gate_w, gate_b, lru_lambda,
                          layer, L, nseq, blk0, lru_col, dconv, nheads)
            y = _pool_call(proj, y, pool_w_b, pool_s3, layer, L, nseq, blk0, pool_col, dconv + dlru)
        h = _outproj_call(h, y, w_out_b, *ln2, layer, alpha)
        if layer + 1 < depth:
            h = _ffn_call(h, *f2, layer, alpha)
        else:
            outs = [_ffn_call(h, *f2, layer, alpha, seq=(L, nseq, blk0, S)).reshape(nseq, S, D)
                    for _, L, nseq, blk0, S in groups]

    y_sample, y_prompt = outs
    return (y_prompt, y_sample)
```

```python
import functools

import jax
import jax.numpy as jnp
from jax import lax
from jax.experimental import pallas as pl
from jax.experimental.pallas import tpu as pltpu

F32 = jnp.float32
BF16 = jnp.bfloat16

N_META = 16
POOL_WINDOWS = (2, 4, 8, 16)
SHORT_CONV_WIDTH = 3
LRU_CONV_WIDTH = 4
LRU_C = 8.0
LN_EPS = 1e-5
LOG2_E = 1.4426950408889634
GELU_C0 = 0.7978845608028654
GELU_C1 = 0.044715

LANES = 128
SUBLANES = 8
BF16_SUBLANES = 16
VMEM_BYTES_V7X = 64 * 1024 * 1024

ROW_TILE = 512
FF_TILE = 512
PROJ_TILE = 1024
MIX_CHUNK = 256
COPY_CHUNK = 256
MARGIN = SUBLANES
SCAN_SEGMENTS = 2 * SUBLANES


def _vmem_limit(block_bytes):
    return int(min(block_bytes + block_bytes // 2 + (8 << 20), VMEM_BYTES_V7X - (6 << 20)))


def _layernorm(z, g, b):
    mu = jnp.mean(z, axis=-1, keepdims=True)
    d = z - mu
    var = jnp.mean(d * d, axis=-1, keepdims=True)
    return d * lax.rsqrt(var + LN_EPS) * g + b


def _ln_kernel(x_ref, g_ref, b_ref, hprev_ref, o_ref):
    del hprev_ref
    o_ref[...] = _layernorm(x_ref[...], g_ref[...], b_ref[...])


def _ln_in_call(x, h, g, b, L, blk0, rows_per_step, row0):
    nseq, S, D = x.shape
    T = h.shape[0]
    nj = S // rows_per_step
    return pl.pallas_call(
        _ln_kernel,
        grid=(nseq, nj),
        in_specs=[pl.BlockSpec((None, rows_per_step, D), lambda s, j: (s, j, 0)),
                  pl.BlockSpec((1, D), lambda s, j: (0, 0)),
                  pl.BlockSpec((1, D), lambda s, j: (0, 0)),
                  pl.BlockSpec(memory_space=pl.ANY)],
        out_specs=pl.BlockSpec((pl.Element(rows_per_step), pl.Element(D)),
                               lambda s, j: (pl.multiple_of((blk0 + s) * L + row0 + j * rows_per_step,
                                                            SUBLANES), 0)),
        out_shape=jax.ShapeDtypeStruct((T, D), F32),
        input_output_aliases={3: 0},
        compiler_params=pltpu.CompilerParams(
            dimension_semantics=("arbitrary", "arbitrary"),
            vmem_limit_bytes=_vmem_limit(4 * rows_per_step * D * 4)),
        name="ln_in",
    )(x, g.reshape(1, D), b.reshape(1, D), h)


def _ffn_kernel(x_ref, wg_hbm, wu_hbm, wd_hbm, g_ref, b_ref, o_ref,
                wg_buf, wu_buf, wd_buf, sem, xb_ref, acc_ref, z_ref, *, nk, n_tiles, layer, alpha):
    i = pl.program_id(0)
    real = i < n_tiles
    ln_steps = 1 << ((nk - 2).bit_length() - 1)
    ln_rows = ROW_TILE // ln_steps

    def chunk_copies(c, slot):
        cols = pl.ds(c * FF_TILE, FF_TILE)
        return (pltpu.make_async_copy(wg_hbm.at[layer, :, cols], wg_buf.at[slot], sem.at[0, slot]),
                pltpu.make_async_copy(wu_hbm.at[layer, :, cols], wu_buf.at[slot], sem.at[1, slot]),
                pltpu.make_async_copy(wd_hbm.at[layer, cols, :], wd_buf.at[slot], sem.at[2, slot]))

    def normalise(rows):
        o_ref[rows] = _layernorm(z_ref[rows], g_ref[...], b_ref[...])

    @pl.when(i == 0)
    def _():
        z_ref[...] = jnp.zeros_like(z_ref)
        for copy in chunk_copies(0, 0):
            copy.start()

    @pl.when(real)
    def _():
        first = (i & 1) if nk % 2 else 0
        xb_ref[...] = x_ref[...].astype(BF16)
        for c in range(nk):
            slot = first ^ (c & 1)
            if c + 1 < nk:
                for copy in chunk_copies(c + 1, 1 - slot):
                    copy.start()
            else:
                @pl.when(i + 1 < n_tiles)
                def _():
                    for copy in chunk_copies(0, 1 - slot):
                        copy.start()
            for copy in chunk_copies(c, slot):
                copy.wait()
            xb = xb_ref[...]
            gate = jnp.dot(xb, wg_buf[slot], preferred_element_type=F32)
            up = jnp.dot(xb, wu_buf[slot], preferred_element_type=F32)
            hidden = (gate * jax.nn.sigmoid(gate) * up).astype(BF16)
            part = jnp.dot(hidden, wd_buf[slot], preferred_element_type=F32)
            if 1 <= c <= ln_steps:
                normalise(pl.ds((c - 1) * ln_rows, ln_rows))
            if c == 0:
                acc_ref[...] = part
            elif c + 1 < nk:
                acc_ref[...] += part
            else:
                z_ref[...] = alpha * x_ref[...] + 0.5 * (acc_ref[...] + part)

    @pl.when(jnp.logical_not(real))
    def _():
        normalise(pl.ds(0, ROW_TILE))


def _ffn_call(h, wg, wu, wd, g, b, layer, alpha, seq=None):
    T, D = h.shape
    F = wg.shape[-1]
    nk = F // FF_TILE
    assert nk >= 3
    if seq is None:
        n_tiles, out_rows = T // ROW_TILE, T
        x_spec = pl.BlockSpec((ROW_TILE, D), lambda i: (jnp.minimum(i, n_tiles - 1), 0))
    else:
        L, nseq, blk0, S = seq
        nj = S // ROW_TILE
        n_tiles, out_rows = nseq * nj, nseq * S

        def x_rows(i):
            t = jnp.minimum(i, n_tiles - 1)
            return (pl.multiple_of((blk0 + t // nj) * L + (t % nj) * ROW_TILE, SUBLANES), 0)

        x_spec = pl.BlockSpec((pl.Element(ROW_TILE), pl.Element(D)), x_rows)
    blocks = (4 * ROW_TILE * D * 4 + 6 * D * FF_TILE * 2 + ROW_TILE * D * (2 + 4 + 4)
              + 4 * ROW_TILE * FF_TILE * 4)
    return pl.pallas_call(
        functools.partial(_ffn_kernel, nk=nk, n_tiles=n_tiles, layer=layer, alpha=alpha),
        grid=(n_tiles + 1,),
        in_specs=[x_spec,
                  pl.BlockSpec(memory_space=pl.ANY),
                  pl.BlockSpec(memory_space=pl.ANY),
                  pl.BlockSpec(memory_space=pl.ANY),
                  pl.BlockSpec((None, 1, D), lambda i: (layer, 0, 0)),
                  pl.BlockSpec((None, 1, D), lambda i: (layer, 0, 0))],
        out_specs=pl.BlockSpec((ROW_TILE, D), lambda i: (jnp.maximum(i - 1, 0), 0)),
        out_shape=jax.ShapeDtypeStruct((out_rows, D), F32),
        scratch_shapes=[pltpu.VMEM((2, D, FF_TILE), BF16), pltpu.VMEM((2, D, FF_TILE), BF16),
                        pltpu.VMEM((2, FF_TILE, D), BF16), pltpu.SemaphoreType.DMA((3, 2)),
                        pltpu.VMEM((ROW_TILE, D), BF16), pltpu.VMEM((ROW_TILE, D), F32),
                        pltpu.VMEM((ROW_TILE, D), F32)],
        compiler_params=pltpu.CompilerParams(
            dimension_semantics=("arbitrary",),
            vmem_limit_bytes=_vmem_limit(blocks)),
        name="ffn",
    )(h, wg, wu, wd, g, b)


def _inproj_kernel(x_ref, w_ref, o_ref):
    xb = x_ref[...].astype(BF16)
    for n in range(o_ref.shape[-1] // PROJ_TILE):
        cols = slice(n * PROJ_TILE, (n + 1) * PROJ_TILE)
        o_ref[:, cols] = jnp.dot(xb, w_ref[:, cols], preferred_element_type=F32)


def _inproj_call(h, w_in, layer):
    T, D = h.shape
    P = w_in.shape[-1]
    blocks = 2 * ROW_TILE * D * 4 + D * P * 2 + 2 * ROW_TILE * P * 4 + ROW_TILE * D * 2
    return pl.pallas_call(
        _inproj_kernel,
        grid=(T // ROW_TILE,),
        in_specs=[pl.BlockSpec((ROW_TILE, D), lambda i: (i, 0)),
                  pl.BlockSpec((None, D, P), lambda i: (layer, 0, 0), pipeline_mode=pl.Buffered(1))],
        out_specs=pl.BlockSpec((ROW_TILE, P), lambda i: (i, 0)),
        out_shape=jax.ShapeDtypeStruct((T, P), F32),
        compiler_params=pltpu.CompilerParams(
            dimension_semantics=("parallel",),
            vmem_limit_bytes=_vmem_limit(blocks)),
        name="in_proj",
    )(h, w_in)


def _outproj_kernel(h_ref, y_ref, w_ref, g_ref, b_ref, o_ref, *, alpha):
    mix = jnp.dot(y_ref[...], w_ref[...], preferred_element_type=F32)
    o_ref[...] = _layernorm(alpha * h_ref[...] + mix, g_ref[...], b_ref[...])


def _outproj_call(h, y, w_out, g, b, layer, alpha):
    T, D = h.shape
    blocks = 4 * ROW_TILE * D * 4 + 2 * ROW_TILE * D * 2 + D * D * 2 + 2 * ROW_TILE * D * 4
    return pl.pallas_call(
        functools.partial(_outproj_kernel, alpha=alpha),
        grid=(T // ROW_TILE,),
        in_specs=[pl.BlockSpec((ROW_TILE, D), lambda i: (i, 0)),
                  pl.BlockSpec((ROW_TILE, D), lambda i: (i, 0)),
                  pl.BlockSpec((None, D, D), lambda i: (layer, 0, 0), pipeline_mode=pl.Buffered(1)),
                  pl.BlockSpec((None, 1, D), lambda i: (layer, 0, 0)),
                  pl.BlockSpec((None, 1, D), lambda i: (layer, 0, 0))],
        out_specs=pl.BlockSpec((ROW_TILE, D), lambda i: (i, 0)),
        out_shape=jax.ShapeDtypeStruct((T, D), F32),
        compiler_params=pltpu.CompilerParams(
            dimension_semantics=("parallel",),
            vmem_limit_bytes=_vmem_limit(blocks)),
        name="out_proj",
    )(h, y, w_out, g, b)


def _fill_padded(dst_ref, L, read):
    lx = L - N_META
    zeros = jnp.zeros((MARGIN, LANES), F32)
    dst_ref[0:MARGIN] = zeros
    dst_ref[MARGIN + L:2 * MARGIN + L] = zeros
    dst_ref[MARGIN:MARGIN + N_META] = read(pl.ds(lx, N_META))

    def body(i, c):
        r = pl.multiple_of(i * COPY_CHUNK, COPY_CHUNK)
        dst = pl.multiple_of(r + (MARGIN + N_META), SUBLANES)
        dst_ref[pl.ds(dst, COPY_CHUNK)] = read(pl.ds(r, COPY_CHUNK))
        return c

    lax.fori_loop(0, lx // COPY_CHUNK, body, 0)


def _for_storage_chunks(L, fn):
    lx = L - N_META
    last = lx // COPY_CHUNK - 1

    def body(i, c):
        r = pl.multiple_of(i * COPY_CHUNK, COPY_CHUNK)
        fn(pl.ds(r, COPY_CHUNK), r + N_META, COPY_CHUNK, False)
        return c

    lax.fori_loop(0, last, body, 0)
    fn(pl.ds(last * COPY_CHUNK, COPY_CHUNK), last * COPY_CHUNK + N_META, COPY_CHUNK, True)
    fn(pl.ds(lx, N_META), 0, N_META, True)


def _for_logical_chunks(L, chunk, fn):
    n_main = L // chunk

    def body(i, c):
        fn(pl.multiple_of(i * chunk, chunk), chunk)
        return c

    lax.fori_loop(0, n_main, body, 0)
    if L % chunk:
        fn(n_main * chunk, L % chunk)


def _conv_kernel(bg_ref, cg_ref, v_ref, w_ref, cb_ref, yprev_ref, o_ref, upad_ref, *, L):
    del yprev_ref
    _fill_padded(upad_ref, L, lambda rows: cg_ref[rows] * v_ref[rows])
    w = w_ref[...]
    cb = cb_ref[...]

    def emit(rows, t0, n, at_end):
        del at_end
        acc = cb
        for k in range(SHORT_CONV_WIDTH):
            off = k - SHORT_CONV_WIDTH // 2
            acc = acc + w[k:k + 1] * upad_ref[pl.ds(t0 + (MARGIN + off), n)]
        o_ref[rows] = (bg_ref[rows] * acc).astype(BF16)

    _for_storage_chunks(L, emit)


def _pool_kernel(u_ref, pw_ref, ps_ref, yprev_ref, o_ref, upad_ref, pooled_ref, *, L):
    del yprev_ref
    _fill_padded(upad_ref, L, lambda rows: u_ref[rows])
    group = pl.program_id(1)

    for gi, win in enumerate(POOL_WINDOWS):
        half = win // 2

        @pl.when(group == gi)
        def _(win=win, half=half):
            def emit(rows, t0, n, at_end):
                total = upad_ref[pl.ds(t0 + (MARGIN - half), n)]
                for j in range(1 - half, half):
                    total = total + upad_ref[pl.ds(t0 + (MARGIN + j), n)]
                if at_end:
                    t = t0 + lax.broadcasted_iota(jnp.int32, (n, LANES), 0)
                    lo = jnp.maximum(t - half, 0)
                    hi = jnp.minimum(t + (half - 1), L - 1)
                    mean = total / (hi - lo + 1).astype(F32)
                else:
                    mean = total * (1.0 / win)
                pooled_ref[rows] = (mean - upad_ref[pl.ds(t0 + MARGIN, n)]).astype(BF16)

            _for_storage_chunks(L, emit)

    mixed = jnp.dot(pooled_ref[...], pw_ref[...], preferred_element_type=F32)
    o_ref[...] = (mixed * ps_ref[...]).astype(BF16)


def _lru_kernel(x_ref, gate_ref, cw_ref, cb_ref, wg_ref, bg_ref, lam_ref, yprev_ref, o_ref,
                xpad_ref, xc_ref, a_ref, b_ref, h_ref, pre0_ref, pre1_ref, carry_ref, *, L):
    del yprev_ref
    seg = L // SCAN_SEGMENTS
    _fill_padded(xpad_ref, L, lambda rows: x_ref[rows])

    cw = cw_ref[...]
    cb = cb_ref[...]
    bg = bg_ref[...]
    neg_lam = -lam_ref[...]
    softplus = jnp.maximum(neg_lam, 0.0) + jnp.log1p(jnp.exp(-jnp.abs(neg_lam)))
    half_c2 = (-0.5 * LRU_C * LOG2_E) * softplus

    def conv(t0, n):
        for d in range(2):
            xc = cb[d:d + 1]
            for k in range(LRU_CONV_WIDTH):
                off = (k - (LRU_CONV_WIDTH - 1)) if d == 0 else ((LRU_CONV_WIDTH - 1) - k)
                xc = xc + cw[d, k:k + 1] * xpad_ref[pl.ds(t0 + (MARGIN + off), n)]
            xc_ref[d, pl.ds(t0, n)] = xc

    _for_logical_chunks(L, MIX_CHUNK, conv)

    def gate_matmuls(t0, n, pre_ref):
        for d in range(2):
            pre_ref[d, 0:n] = jnp.dot(xc_ref[d, pl.ds(t0, n)].astype(BF16), wg_ref[d],
                                      preferred_element_type=F32)

    def finish(t0, n, pre_ref):
        for d in range(2):
            pre = pre_ref[d, 0:n] + bg[d:d + 1]
            t_r = jnp.tanh(pre[:, :LANES])
            t_i = jnp.tanh(pre[:, LANES:])
            hc = half_c2[d:d + 1]
            a = jnp.exp2(hc + hc * t_r)
            gated = (0.5 * xc_ref[d, pl.ds(t0, n)]) * (1.0 + t_i)
            a_ref[d, pl.ds(t0, n)] = a
            v = 1.0 - a * a
            b_ref[d, pl.ds(t0, n)] = jnp.where(v > 0.0, v * lax.rsqrt(v), 0.0) * gated

    n_main = L // MIX_CHUNK
    assert n_main % 2 == 0
    gate_matmuls(0, MIX_CHUNK, pre0_ref)

    def gates_body(i, c):
        c0 = pl.multiple_of(2 * i * MIX_CHUNK, MIX_CHUNK)
        c1 = pl.multiple_of(c0 + MIX_CHUNK, MIX_CHUNK)
        nxt = pl.multiple_of(jnp.minimum(2 * i + 2, n_main - 1) * MIX_CHUNK, MIX_CHUNK)
        gate_matmuls(c1, MIX_CHUNK, pre1_ref)
        finish(c0, MIX_CHUNK, pre0_ref)
        gate_matmuls(nxt, MIX_CHUNK, pre0_ref)
        finish(c1, MIX_CHUNK, pre1_ref)
        return c

    lax.fori_loop(0, n_main // 2, gates_body, 0)
    if L % MIX_CHUNK:
        t0, n = n_main * MIX_CHUNK, L % MIX_CHUNK
        gate_matmuls(t0, n, pre0_ref)
        finish(t0, n, pre0_ref)

    nv = SCAN_SEGMENTS // SUBLANES

    def seg_rows(j, t):
        return pl.ds(j * SUBLANES * seg + t, SUBLANES, stride=seg)

    def ends_body(t, c):
        out = []
        for d in range(2):
            tt = t if d == 0 else seg - 1 - t
            for j in range(nv):
                h, p = c[2 * (d * nv + j)], c[2 * (d * nv + j) + 1]
                rows = seg_rows(j, tt)
                a = a_ref[d, rows]
                out += [a * h + b_ref[d, rows], p * a]
        return tuple(out)

    zero = jnp.zeros((SUBLANES, LANES), F32)
    one = jnp.ones((SUBLANES, LANES), F32)
    ends = lax.fori_loop(0, seg, ends_body, (zero, one) * (2 * nv), unroll=4)

    S = SCAN_SEGMENTS
    for d in range(2):
        for j in range(nv):
            carry_ref[d, pl.ds(j * SUBLANES, SUBLANES)] = ends[2 * (d * nv + j)]
            carry_ref[d, pl.ds(S + j * SUBLANES, SUBLANES)] = ends[2 * (d * nv + j) + 1]
        state = jnp.zeros((1, LANES), F32)
        for s in (range(S) if d == 0 else reversed(range(S))):
            carry_ref[d, 2 * S + s:2 * S + s + 1] = state
            state = carry_ref[d, S + s:S + s + 1] * state + carry_ref[d, s:s + 1]
    cin = tuple(carry_ref[d, pl.ds(2 * S + j * SUBLANES, SUBLANES)] for d in range(2) for j in range(nv))

    def scan_body(t, c):
        out = []
        for d in range(2):
            tt = t if d == 0 else seg - 1 - t
            for j in range(nv):
                rows = seg_rows(j, tt)
                h = a_ref[d, rows] * c[d * nv + j] + b_ref[d, rows]
                h_ref[d, rows] = h
                out.append(h)
        return tuple(out)

    lax.fori_loop(0, seg, scan_body, cin, unroll=4)

    def emit(rows, t0, n, at_end):
        del at_end
        g = gate_ref[rows]
        t = jnp.tanh(g * (GELU_C0 + (GELU_C0 * GELU_C1) * (g * g)))
        hsum = h_ref[0, pl.ds(t0, n)] + h_ref[1, pl.ds(t0, n)]
        o_ref[rows] = (((0.5 * g) * (1.0 + t)) * hsum).astype(BF16)

    _for_storage_chunks(L, emit)


def _mixer_params(nseq, nblk, scratch_bytes):
    return dict(
        grid=(nseq, nblk),
        compiler_params=pltpu.CompilerParams(
            dimension_semantics=("parallel", "arbitrary"),
            vmem_limit_bytes=_vmem_limit(scratch_bytes)),
    )


def _conv_call(proj, y, conv_w, conv_b, layer, L, nseq, blk0, dconv):
    nb = dconv // LANES
    T, D = y.shape
    seq = lambda base: pl.BlockSpec((L, LANES), lambda b, j: (b + blk0, base + j))
    return pl.pallas_call(
        functools.partial(_conv_kernel, L=L),
        in_specs=[seq(0), seq(nb), seq(2 * nb),
                  pl.BlockSpec((None, SHORT_CONV_WIDTH, LANES), lambda b, j: (layer, 0, j)),
                  pl.BlockSpec((None, 1, LANES), lambda b, j: (layer, 0, j)),
                  pl.BlockSpec(memory_space=pl.ANY)],
        out_specs=pl.BlockSpec((L, LANES), lambda b, j: (b + blk0, j)),
        out_shape=jax.ShapeDtypeStruct((T, D), BF16),
        scratch_shapes=[pltpu.VMEM((L + 2 * MARGIN, LANES), F32)],
        input_output_aliases={5: 0},
        name="mix_conv",
        **_mixer_params(nseq, nb, 9 * L * LANES * 4),
    )(proj, proj, proj, conv_w, conv_b, y)


def _lru_call(proj, y, cw, cb, wg, bg, lam, layer, L, nseq, blk0, col0, out0, nheads):
    T, D = y.shape
    c0 = col0 // LANES
    return pl.pallas_call(
        functools.partial(_lru_kernel, L=L),
        in_specs=[pl.BlockSpec((L, LANES), lambda b, j: (b + blk0, c0 + j)),
                  pl.BlockSpec((L, LANES), lambda b, j: (b + blk0, c0 + nheads + j)),
                  pl.BlockSpec((None, 2, LRU_CONV_WIDTH, LANES), lambda b, j: (layer, 0, 0, j)),
                  pl.BlockSpec((None, 2, LANES), lambda b, j: (layer, 0, j)),
                  pl.BlockSpec((None, None, 2, LANES, 2 * LANES), lambda b, j: (layer, j, 0, 0, 0)),
                  pl.BlockSpec((None, None, 2, 2 * LANES), lambda b, j: (layer, j, 0, 0)),
                  pl.BlockSpec((None, 2, LANES), lambda b, j: (layer, 0, j)),
                  pl.BlockSpec(memory_space=pl.ANY)],
        out_specs=pl.BlockSpec((L, LANES), lambda b, j: (b + blk0, out0 // LANES + j)),
        out_shape=jax.ShapeDtypeStruct((T, D), BF16),
        scratch_shapes=[pltpu.VMEM((L + 2 * MARGIN, LANES), F32)]
        + [pltpu.VMEM((2, L, LANES), F32)] * 4
        + [pltpu.VMEM((2, MIX_CHUNK, 2 * LANES), F32)] * 2
        + [pltpu.VMEM((2, 3 * SCAN_SEGMENTS, LANES), F32)],
        input_output_aliases={7: 0},
        name="mix_lru",
        **_mixer_params(nseq, nheads, 15 * L * LANES * 4),
    )(proj, proj, cw, cb, wg, bg, lam, y)


def _pool_call(proj, y, pool_w, pool_scale, layer, L, nseq, blk0, col0, out0):
    T, D = y.shape
    ng = len(POOL_WINDOWS)
    return pl.pallas_call(
        functools.partial(_pool_kernel, L=L),
        in_specs=[pl.BlockSpec((L, LANES), lambda b, j: (b + blk0, col0 // LANES + j)),
                  pl.BlockSpec((None, None, LANES, LANES), lambda b, j: (layer, j, 0, 0)),
                  pl.BlockSpec((None, 1, LANES), lambda b, j: (layer, 0, j)),
                  pl.BlockSpec(memory_space=pl.ANY)],
        out_specs=pl.BlockSpec((L, LANES), lambda b, j: (b + blk0, out0 // LANES + j)),
        out_shape=jax.ShapeDtypeStruct((T, D), BF16),
        scratch_shapes=[pltpu.VMEM((L + 2 * MARGIN, LANES), F32), pltpu.VMEM((L, LANES), BF16)],
        input_output_aliases={3: 0},
        name="mix_pool",
        **_mixer_params(nseq, ng, 5 * L * LANES * 4),
    )(proj, pool_w, pool_scale, y)


def kernel(x_prompt, x_sample, meta_tokens, ln_in_g, ln_in_b, ffn1_w_gate, ffn1_w_up, ffn1_w_down, ln1_g, ln1_b, w_in, conv_w, conv_b, lru_conv_w, lru_conv_b, lru_w_a, lru_b_a, lru_w_x, lru_b_x, lru_lambda, pool_w, pool_scale, w_out, ln2_g, ln2_b, ffn2_w_gate, ffn2_w_up, ffn2_w_down, ln3_g, ln3_b):
    depth = ffn1_w_gate.shape[0]
    alpha = float((2 * depth) ** 0.25)
    bp, sp, D = x_prompt.shape
    bs, ss, _ = x_sample.shape
    lp, ls = sp + N_META, ss + N_META
    dconv = conv_w.shape[-1]
    dlru = lru_lambda.shape[-1]
    dpool = pool_scale.shape[-1]
    nheads = lru_w_a.shape[2]
    assert D == dconv + dlru + dpool and dlru // nheads == LANES and dpool // len(POOL_WINDOWS) == LANES
    assert dconv % LANES == 0 and sp % ROW_TILE == 0 and ss % ROW_TILE == 0
    assert lp % SCAN_SEGMENTS == 0 and ls % SCAN_SEGMENTS == 0
    assert max(POOL_WINDOWS) // 2 <= min(N_META, MARGIN)

    ts = bs * ls
    blk0_p = -(-ts // lp)
    off_p = blk0_p * lp
    t_real = off_p + bp * lp
    t_pad = -(-t_real // ROW_TILE) * ROW_TILE
    groups = ((x_sample, ls, bs, 0, ss), (x_prompt, lp, bp, blk0_p, sp))

    bf = lambda w: w.astype(BF16)
    row3 = lambda p: p.reshape(depth, 1, -1)
    f1 = (bf(ffn1_w_gate), bf(ffn1_w_up), bf(ffn1_w_down), row3(ln1_g), row3(ln1_b))
    f2 = (bf(ffn2_w_gate), bf(ffn2_w_up), bf(ffn2_w_down), row3(ln3_g), row3(ln3_b))
    w_in_b, w_out_b = bf(w_in), bf(w_out)
    gate_w = bf(0.5 * jnp.concatenate([lru_w_a, lru_w_x], axis=-1)).transpose(0, 2, 1, 3, 4)
    gate_b = 0.5 * jnp.concatenate([lru_b_a.reshape(depth, 2, nheads, LANES),
                                    lru_b_x.reshape(depth, 2, nheads, LANES)], axis=-1).transpose(0, 2, 1, 3)
    pool_w_b = bf(pool_w)
    conv_b3, pool_s3 = row3(conv_b), row3(pool_scale)
    ln2 = (row3(ln2_g), row3(ln2_b))
    lru_col, pool_col = 3 * dconv, 3 * dconv + 2 * dlru

    h = jnp.zeros((t_pad, D), F32)
    y = jnp.zeros((t_pad, D), BF16)
    for x, L, nseq, blk0, S in groups:
        h = _ln_in_call(x, h, ln_in_g, ln_in_b, L, blk0, ROW_TILE, 0)
        meta = jnp.broadcast_to(meta_tokens[None], (nseq, N_META, D))
        h = _ln_in_call(meta, h, ln_in_g, ln_in_b, L, blk0, N_META, S)

    outs = []
    for layer in range(depth):
        h = _ffn_call(h, *f1, layer, alpha)
        proj = _inproj_call(h, w_in_b, layer)
        for _, L, nseq, blk0, _ in groups:
            y = _conv_call(proj, y, conv_w, conv_b3, layer, L, nseq, blk0, dconv)
            y = _lru_call(proj, y, lru_conv_w, lru_conv_b, gate_w, gate_b, lru_lambda,
                          layer, L, nseq, blk0, lru_col, dconv, nheads)
            y = _pool_call(proj, y, pool_w_b, pool_s3, layer, L, nseq, blk0, pool_col, dconv + dlru)
        h = _outproj_call(h, y, w_out_b, *ln2, layer, alpha)
        if layer + 1 < depth:
            h = _ffn_call(h, *f2, layer, alpha)
        else:
            outs = [_ffn_call(h, *f2, layer, alpha, seq=(L, nseq, blk0, S)).reshape(nseq, S, D)
                    for _, L, nseq, blk0, S in groups]

    y_sample, y_prompt = outs
    return (y_prompt, y_sample)
```

```python
import functools

import jax
import jax.numpy as jnp
from jax import lax
from jax.experimental import pallas as pl
from jax.experimental.pallas import tpu as pltpu

F32 = jnp.float32
BF16 = jnp.bfloat16

N_META = 16
POOL_WINDOWS = (2, 4, 8, 16)
SHORT_CONV_WIDTH = 3
LRU_CONV_WIDTH = 4
LRU_C = 8.0
LN_EPS = 1e-5
LOG2_E = 1.4426950408889634
GELU_C0 = 0.7978845608028654
GELU_C1 = 0.044715

LANES = 128
SUBLANES = 8
BF16_SUBLANES = 16
VMEM_BYTES_V7X = 64 * 1024 * 1024

ROW_TILE = 512
PROJ_ROWS = 512
SEQ_TILE = 512
FF_TILE = 512
PROJ_TILE = 1024
MIX_CHUNK = 512
COPY_CHUNK = 256
MARGIN = SUBLANES
SCAN_SEGMENTS = 2 * SUBLANES


def _vmem_limit(block_bytes):
    return int(min(block_bytes + block_bytes // 2 + (8 << 20), VMEM_BYTES_V7X - (6 << 20)))


def _layernorm(z, g, b):
    mu = jnp.mean(z, axis=-1, keepdims=True)
    d = z - mu
    var = jnp.mean(d * d, axis=-1, keepdims=True)
    return d * lax.rsqrt(var + LN_EPS) * g + b


def _ln_kernel(x_ref, g_ref, b_ref, hprev_ref, o_ref):
    del hprev_ref
    o_ref[...] = _layernorm(x_ref[...], g_ref[...], b_ref[...])


def _ln_in_call(x, h, g, b, L, blk0, rows_per_step, row0):
    nseq, S, D = x.shape
    T = h.shape[0]
    nj = S // rows_per_step
    return pl.pallas_call(
        _ln_kernel,
        grid=(nseq, nj),
        in_specs=[pl.BlockSpec((None, rows_per_step, D), lambda s, j: (s, j, 0)),
                  pl.BlockSpec((1, D), lambda s, j: (0, 0)),
                  pl.BlockSpec((1, D), lambda s, j: (0, 0)),
                  pl.BlockSpec(memory_space=pl.ANY)],
        out_specs=pl.BlockSpec((pl.Element(rows_per_step), pl.Element(D)),
                               lambda s, j: (pl.multiple_of((blk0 + s) * L + row0 + j * rows_per_step,
                                                            SUBLANES), 0)),
        out_shape=jax.ShapeDtypeStruct((T, D), F32),
        input_output_aliases={3: 0},
        compiler_params=pltpu.CompilerParams(
            dimension_semantics=("arbitrary", "arbitrary"),
            vmem_limit_bytes=_vmem_limit(4 * rows_per_step * D * 4)),
        name="ln_in",
    )(x, g.reshape(1, D), b.reshape(1, D), h)


def _ffn_kernel(x_ref, wg_hbm, wu_hbm, wd_hbm, g_ref, b_ref, o_ref,
                wg_buf, wu_buf, wd_buf, sem, xb_ref, acc_ref, z_ref, *, nk, n_tiles, layer, alpha):
    i = pl.program_id(0)
    real = i < n_tiles
    ln_steps = 1 << ((nk - 2).bit_length() - 1)
    row_tile = x_ref.shape[0]
    ln_rows = row_tile // ln_steps

    def chunk_copies(c, slot):
        cols = pl.ds(c * FF_TILE, FF_TILE)
        return (pltpu.make_async_copy(wg_hbm.at[layer, :, cols], wg_buf.at[slot], sem.at[0, slot]),
                pltpu.make_async_copy(wu_hbm.at[layer, :, cols], wu_buf.at[slot], sem.at[1, slot]),
                pltpu.make_async_copy(wd_hbm.at[layer, cols, :], wd_buf.at[slot], sem.at[2, slot]))

    def normalise(rows):
        o_ref[rows] = _layernorm(z_ref[rows], g_ref[...], b_ref[...])

    @pl.when(i == 0)
    def _():
        z_ref[...] = jnp.zeros_like(z_ref)
        for copy in chunk_copies(0, 0):
            copy.start()

    @pl.when(real)
    def _():
        first = (i & 1) if nk % 2 else 0
        xb_ref[...] = x_ref[...].astype(BF16)
        for c in range(nk):
            slot = first ^ (c & 1)
            if c + 1 < nk:
                for copy in chunk_copies(c + 1, 1 - slot):
                    copy.start()
            else:
                @pl.when(i + 1 < n_tiles)
                def _():
                    for copy in chunk_copies(0, 1 - slot):
                        copy.start()
            for copy in chunk_copies(c, slot):
                copy.wait()
            xb = xb_ref[...]
            gate = jnp.dot(xb, wg_buf[slot], preferred_element_type=F32)
            up = jnp.dot(xb, wu_buf[slot], preferred_element_type=F32)
            hidden = (gate * jax.nn.sigmoid(gate) * up).astype(BF16)
            part = jnp.dot(hidden, wd_buf[slot], preferred_element_type=F32)
            if 1 <= c <= ln_steps:
                normalise(pl.ds((c - 1) * ln_rows, ln_rows))
            if c == 0:
                acc_ref[...] = part
            elif c + 1 < nk:
                acc_ref[...] += part
            else:
                z_ref[...] = alpha * x_ref[...] + 0.5 * (acc_ref[...] + part)

    @pl.when(jnp.logical_not(real))
    def _():
        normalise(pl.ds(0, row_tile))


def _ffn_call(h, wg, wu, wd, g, b, layer, alpha, seq=None):
    T, D = h.shape
    F = wg.shape[-1]
    nk = F // FF_TILE
    assert nk >= 3
    if seq is None:
        rt = ROW_TILE
        n_tiles, out_rows = T // rt, T
        x_spec = pl.BlockSpec((rt, D), lambda i: (jnp.minimum(i, n_tiles - 1), 0))
    else:
        L, nseq, blk0, S = seq
        rt = SEQ_TILE
        nj = S // rt
        n_tiles, out_rows = nseq * nj, nseq * S

        def x_rows(i):
            t = jnp.minimum(i, n_tiles - 1)
            return (pl.multiple_of((blk0 + t // nj) * L + (t % nj) * rt, SUBLANES), 0)

        x_spec = pl.BlockSpec((pl.Element(rt), pl.Element(D)), x_rows)
    blocks = 4 * rt * D * 4 + 6 * D * FF_TILE * 2 + rt * D * (2 + 4 + 4) + 3 * rt * FF_TILE * 4
    return pl.pallas_call(
        functools.partial(_ffn_kernel, nk=nk, n_tiles=n_tiles, layer=layer, alpha=alpha),
        grid=(n_tiles + 1,),
        in_specs=[x_spec,
                  pl.BlockSpec(memory_space=pl.ANY),
                  pl.BlockSpec(memory_space=pl.ANY),
                  pl.BlockSpec(memory_space=pl.ANY),
                  pl.BlockSpec((None, 1, D), lambda i: (layer, 0, 0)),
                  pl.BlockSpec((None, 1, D), lambda i: (layer, 0, 0))],
        out_specs=pl.BlockSpec((rt, D), lambda i: (jnp.maximum(i - 1, 0), 0)),
        out_shape=jax.ShapeDtypeStruct((out_rows, D), F32),
        scratch_shapes=[pltpu.VMEM((2, D, FF_TILE), BF16), pltpu.VMEM((2, D, FF_TILE), BF16),
                        pltpu.VMEM((2, FF_TILE, D), BF16), pltpu.SemaphoreType.DMA((3, 2)),
                        pltpu.VMEM((rt, D), BF16), pltpu.VMEM((rt, D), F32),
                        pltpu.VMEM((rt, D), F32)],
        compiler_params=pltpu.CompilerParams(
            dimension_semantics=("arbitrary",),
            vmem_limit_bytes=_vmem_limit(blocks)),
        name="ffn",
    )(h, wg, wu, wd, g, b)


def _inproj_kernel(x_ref, w_ref, o_ref):
    xb = x_ref[...].astype(BF16)
    for n in range(o_ref.shape[-1] // PROJ_TILE):
        cols = slice(n * PROJ_TILE, (n + 1) * PROJ_TILE)
        o_ref[:, cols] = jnp.dot(xb, w_ref[:, cols], preferred_element_type=F32)


def _inproj_call(h, w_in, layer):
    T, D = h.shape
    P = w_in.shape[-1]
    blocks = 2 * PROJ_ROWS * D * 4 + D * P * 2 + 2 * PROJ_ROWS * P * 4 + PROJ_ROWS * D * 2
    return pl.pallas_call(
        _inproj_kernel,
        grid=(T // PROJ_ROWS,),
        in_specs=[pl.BlockSpec((PROJ_ROWS, D), lambda i: (i, 0)),
                  pl.BlockSpec((None, D, P), lambda i: (layer, 0, 0), pipeline_mode=pl.Buffered(1))],
        out_specs=pl.BlockSpec((PROJ_ROWS, P), lambda i: (i, 0)),
        out_shape=jax.ShapeDtypeStruct((T, P), F32),
        compiler_params=pltpu.CompilerParams(
            dimension_semantics=("parallel",),
            vmem_limit_bytes=_vmem_limit(blocks)),
        name="in_proj",
    )(h, w_in)


def _outproj_kernel(h_ref, y_ref, w_ref, g_ref, b_ref, o_ref, *, alpha):
    mix = jnp.dot(y_ref[...], w_ref[...], preferred_element_type=F32)
    o_ref[...] = _layernorm(alpha * h_ref[...] + mix, g_ref[...], b_ref[...])


def _outproj_call(h, y, w_out, g, b, layer, alpha):
    T, D = h.shape
    rt = PROJ_ROWS
    blocks = 4 * rt * D * 4 + 2 * rt * D * 2 + D * D * 2 + 2 * rt * D * 4
    return pl.pallas_call(
        functools.partial(_outproj_kernel, alpha=alpha),
        grid=(T // rt,),
        in_specs=[pl.BlockSpec((rt, D), lambda i: (i, 0)),
                  pl.BlockSpec((rt, D), lambda i: (i, 0)),
                  pl.BlockSpec((None, D, D), lambda i: (layer, 0, 0), pipeline_mode=pl.Buffered(1)),
                  pl.BlockSpec((None, 1, D), lambda i: (layer, 0, 0)),
                  pl.BlockSpec((None, 1, D), lambda i: (layer, 0, 0))],
        out_specs=pl.BlockSpec((rt, D), lambda i: (i, 0)),
        out_shape=jax.ShapeDtypeStruct((T, D), F32),
        compiler_params=pltpu.CompilerParams(
            dimension_semantics=("parallel",),
            vmem_limit_bytes=_vmem_limit(blocks)),
        name="out_proj",
    )(h, y, w_out, g, b)


def _fill_padded(dst_ref, L, read):
    lx = L - N_META
    zeros = jnp.zeros((MARGIN, LANES), F32)
    dst_ref[0:MARGIN] = zeros
    dst_ref[MARGIN + L:2 * MARGIN + L] = zeros
    dst_ref[MARGIN:MARGIN + N_META] = read(pl.ds(lx, N_META))

    def body(i, c):
        r = pl.multiple_of(i * COPY_CHUNK, COPY_CHUNK)
        dst = pl.multiple_of(r + (MARGIN + N_META), SUBLANES)
        dst_ref[pl.ds(dst, COPY_CHUNK)] = read(pl.ds(r, COPY_CHUNK))
        return c

    lax.fori_loop(0, lx // COPY_CHUNK, body, 0)


def _for_storage_chunks(L, fn):
    lx = L - N_META
    last = lx // COPY_CHUNK - 1

    def body(i, c):
        r = pl.multiple_of(i * COPY_CHUNK, COPY_CHUNK)
        fn(pl.ds(r, COPY_CHUNK), r + N_META, COPY_CHUNK, False)
        return c

    lax.fori_loop(0, last, body, 0)
    fn(pl.ds(last * COPY_CHUNK, COPY_CHUNK), last * COPY_CHUNK + N_META, COPY_CHUNK, True)
    fn(pl.ds(lx, N_META), 0, N_META, True)


def _for_logical_chunks(L, chunk, fn):
    n_main = L // chunk

    def body(i, c):
        fn(pl.multiple_of(i * chunk, chunk), chunk)
        return c

    lax.fori_loop(0, n_main, body, 0)
    if L % chunk:
        fn(n_main * chunk, L % chunk)


def _conv_kernel(bg_ref, cg_ref, v_ref, w_ref, cb_ref, yprev_ref, o_ref, upad_ref, *, L):
    del yprev_ref
    _fill_padded(upad_ref, L, lambda rows: cg_ref[rows] * v_ref[rows])
    w = w_ref[...]
    cb = cb_ref[...]

    def emit(rows, t0, n, at_end):
        del at_end
        acc = cb
        for k in range(SHORT_CONV_WIDTH):
            off = k - SHORT_CONV_WIDTH // 2
            acc = acc + w[k:k + 1] * upad_ref[pl.ds(t0 + (MARGIN + off), n)]
        o_ref[rows] = (bg_ref[rows] * acc).astype(BF16)

    _for_storage_chunks(L, emit)


def _pool_kernel(u_ref, pw_ref, ps_ref, yprev_ref, o_ref, upad_ref, pooled_ref, *, L):
    del yprev_ref
    _fill_padded(upad_ref, L, lambda rows: u_ref[rows])
    group = pl.program_id(1)

    for gi, win in enumerate(POOL_WINDOWS):
        half = win // 2

        @pl.when(group == gi)
        def _(win=win, half=half):
            def emit(rows, t0, n, at_end):
                total = upad_ref[pl.ds(t0 + (MARGIN - half), n)]
                for j in range(1 - half, half):
                    total = total + upad_ref[pl.ds(t0 + (MARGIN + j), n)]
                if at_end:
                    t = t0 + lax.broadcasted_iota(jnp.int32, (n, LANES), 0)
                    lo = jnp.maximum(t - half, 0)
                    hi = jnp.minimum(t + (half - 1), L - 1)
                    mean = total / (hi - lo + 1).astype(F32)
                else:
                    mean = total * (1.0 / win)
                pooled_ref[rows] = (mean - upad_ref[pl.ds(t0 + MARGIN, n)]).astype(BF16)

            _for_storage_chunks(L, emit)

    mixed = jnp.dot(pooled_ref[...], pw_ref[...], preferred_element_type=F32)
    o_ref[...] = (mixed * ps_ref[...]).astype(BF16)


def _lru_kernel(x_ref, gate_ref, cw_ref, cb_ref, wg_ref, bg_ref, lam_ref, yprev_ref, o_ref,
                xpad_ref, xc_ref, a_ref, b_ref, h_ref, pre0_ref, pre1_ref, carry_ref, *, L):
    del yprev_ref
    seg = L // SCAN_SEGMENTS
    _fill_padded(xpad_ref, L, lambda rows: x_ref[rows])

    cw = cw_ref[...]
    cb = cb_ref[...]
    bg = bg_ref[...]
    neg_lam = -lam_ref[...]
    softplus = jnp.maximum(neg_lam, 0.0) + jnp.log1p(jnp.exp(-jnp.abs(neg_lam)))
    half_c2 = (-0.5 * LRU_C * LOG2_E) * softplus

    def conv(t0, n):
        for d in range(2):
            xc = cb[d:d + 1]
            for k in range(LRU_CONV_WIDTH):
                off = (k - (LRU_CONV_WIDTH - 1)) if d == 0 else ((LRU_CONV_WIDTH - 1) - k)
                xc = xc + cw[d, k:k + 1] * xpad_ref[pl.ds(t0 + (MARGIN + off), n)]
            xc_ref[d, pl.ds(t0, n)] = xc

    _for_logical_chunks(L, MIX_CHUNK, conv)

    def gate_matmuls(t0, n, pre_ref):
        for d in range(2):
            pre_ref[d, 0:n] = jnp.dot(xc_ref[d, pl.ds(t0, n)].astype(BF16), wg_ref[d],
                                      preferred_element_type=F32)

    def finish(t0, n, pre_ref):
        for d in range(2):
            pre = pre_ref[d, 0:n] + bg[d:d + 1]
            t_r = jnp.tanh(pre[:, :LANES])
            t_i = jnp.tanh(pre[:, LANES:])
            hc = half_c2[d:d + 1]
            a = jnp.exp2(hc + hc * t_r)
            gated = (0.5 * xc_ref[d, pl.ds(t0, n)]) * (1.0 + t_i)
            a_ref[d, pl.ds(t0, n)] = a
            v = 1.0 - a * a
            b_ref[d, pl.ds(t0, n)] = jnp.where(v > 0.0, v * lax.rsqrt(v), 0.0) * gated

    n_main = L // MIX_CHUNK
    assert n_main % 2 == 0
    gate_matmuls(0, MIX_CHUNK, pre0_ref)

    def gates_body(i, c):
        c0 = pl.multiple_of(2 * i * MIX_CHUNK, MIX_CHUNK)
        c1 = pl.multiple_of(c0 + MIX_CHUNK, MIX_CHUNK)
        nxt = pl.multiple_of(jnp.minimum(2 * i + 2, n_main - 1) * MIX_CHUNK, MIX_CHUNK)
        gate_matmuls(c1, MIX_CHUNK, pre1_ref)
        finish(c0, MIX_CHUNK, pre0_ref)
        gate_matmuls(nxt, MIX_CHUNK, pre0_ref)
        finish(c1, MIX_CHUNK, pre1_ref)
        return c

    lax.fori_loop(0, n_main // 2, gates_body, 0)
    if L % MIX_CHUNK:
        t0, n = n_main * MIX_CHUNK, L % MIX_CHUNK
        gate_matmuls(t0, n, pre0_ref)
        finish(t0, n, pre0_ref)

    nv = SCAN_SEGMENTS // SUBLANES

    def seg_rows(j, t):
        return pl.ds(j * SUBLANES * seg + t, SUBLANES, stride=seg)

    def ends_body(t, c):
        out = []
        for d in range(2):
            tt = t if d == 0 else seg - 1 - t
            for j in range(nv):
                h, p = c[2 * (d * nv + j)], c[2 * (d * nv + j) + 1]
                rows = seg_rows(j, tt)
                a = a_ref[d, rows]
                out += [a * h + b_ref[d, rows], p * a]
        return tuple(out)

    zero = jnp.zeros((SUBLANES, LANES), F32)
    one = jnp.ones((SUBLANES, LANES), F32)
    ends = lax.fori_loop(0, seg, ends_body, (zero, one) * (2 * nv), unroll=4)

    S = SCAN_SEGMENTS
    for d in range(2):
        for j in range(nv):
            carry_ref[d, pl.ds(j * SUBLANES, SUBLANES)] = ends[2 * (d * nv + j)]
            carry_ref[d, pl.ds(S + j * SUBLANES, SUBLANES)] = ends[2 * (d * nv + j) + 1]
        state = jnp.zeros((1, LANES), F32)
        for s in (range(S) if d == 0 else reversed(range(S))):
            carry_ref[d, 2 * S + s:2 * S + s + 1] = state
            state = carry_ref[d, S + s:S + s + 1] * state + carry_ref[d, s:s + 1]
    cin = tuple(carry_ref[d, pl.ds(2 * S + j * SUBLANES, SUBLANES)] for d in range(2) for j in range(nv))

    def scan_body(t, c):
        out = []
        for d in range(2):
            tt = t if d == 0 else seg - 1 - t
            for j in range(nv):
                rows = seg_rows(j, tt)
                h = a_ref[d, rows] * c[d * nv + j] + b_ref[d, rows]
                h_ref[d, rows] = h
                out.append(h)
        return tuple(out)

    lax.fori_loop(0, seg, scan_body, cin, unroll=4)

    def emit(rows, t0, n, at_end):
        del at_end
        g = gate_ref[rows]
        t = jnp.tanh(g * (GELU_C0 + (GELU_C0 * GELU_C1) * (g * g)))
        hsum = h_ref[0, pl.ds(t0, n)] + h_ref[1, pl.ds(t0, n)]
        o_ref[rows] = (((0.5 * g) * (1.0 + t)) * hsum).astype(BF16)

    _for_storage_chunks(L, emit)


def _mixer_params(nseq, nblk, scratch_bytes):
    return dict(
        grid=(nseq, nblk),
        compiler_params=pltpu.CompilerParams(
            dimension_semantics=("parallel", "arbitrary"),
            vmem_limit_bytes=_vmem_limit(scratch_bytes)),
    )


def _conv_call(proj, y, conv_w, conv_b, layer, L, nseq, blk0, dconv):
    nb = dconv // LANES
    T, D = y.shape
    seq = lambda base: pl.BlockSpec((L, LANES), lambda b, j: (b + blk0, base + j))
    return pl.pallas_call(
        functools.partial(_conv_kernel, L=L),
        in_specs=[seq(0), seq(nb), seq(2 * nb),
                  pl.BlockSpec((None, SHORT_CONV_WIDTH, LANES), lambda b, j: (layer, 0, j)),
                  pl.BlockSpec((None, 1, LANES), lambda b, j: (layer, 0, j)),
                  pl.BlockSpec(memory_space=pl.ANY)],
        out_specs=pl.BlockSpec((L, LANES), lambda b, j: (b + blk0, j)),
        out_shape=jax.ShapeDtypeStruct((T, D), BF16),
        scratch_shapes=[pltpu.VMEM((L + 2 * MARGIN, LANES), F32)],
        input_output_aliases={5: 0},
        name="mix_conv",
        **_mixer_params(nseq, nb, 9 * L * LANES * 4),
    )(proj, proj, proj, conv_w, conv_b, y)


def _lru_call(proj, y, cw, cb, wg, bg, lam, layer, L, nseq, blk0, col0, out0, nheads):
    T, D = y.shape
    c0 = col0 // LANES
    return pl.pallas_call(
        functools.partial(_lru_kernel, L=L),
        in_specs=[pl.BlockSpec((L, LANES), lambda b, j: (b + blk0, c0 + j)),
                  pl.BlockSpec((L, LANES), lambda b, j: (b + blk0, c0 + nheads + j)),
                  pl.BlockSpec((None, 2, LRU_CONV_WIDTH, LANES), lambda b, j: (layer, 0, 0, j)),
                  pl.BlockSpec((None, 2, LANES), lambda b, j: (layer, 0, j)),
                  pl.BlockSpec((None, None, 2, LANES, 2 * LANES), lambda b, j: (layer, j, 0, 0, 0)),
                  pl.BlockSpec((None, None, 2, 2 * LANES), lambda b, j: (layer, j, 0, 0)),
                  pl.BlockSpec((None, 2, LANES), lambda b, j: (layer, 0, j)),
                  pl.BlockSpec(memory_space=pl.ANY)],
        out_specs=pl.BlockSpec((L, LANES), lambda b, j: (b + blk0, out0 // LANES + j)),
        out_shape=jax.ShapeDtypeStruct((T, D), BF16),
        scratch_shapes=[pltpu.VMEM((L + 2 * MARGIN, LANES), F32)]
        + [pltpu.VMEM((2, L, LANES), F32)] * 4
        + [pltpu.VMEM((2, MIX_CHUNK, 2 * LANES), F32)] * 2
        + [pltpu.VMEM((2, 3 * SCAN_SEGMENTS, LANES), F32)],
        input_output_aliases={7: 0},
        name="mix_lru",
        **_mixer_params(nseq, nheads, 15 * L * LANES * 4),
    )(proj, proj, cw, cb, wg, bg, lam, y)


def _pool_call(proj, y, pool_w, pool_scale, layer, L, nseq, blk0, col0, out0):
    T, D = y.shape
    ng = len(POOL_WINDOWS)
    return pl.pallas_call(
        functools.partial(_pool_kernel, L=L),
        in_specs=[pl.BlockSpec((L, LANES), lambda b, j: (b + blk0, col0 // LANES + j)),
                  pl.BlockSpec((None, None, LANES, LANES), lambda b, j: (layer, j, 0, 0)),
                  pl.BlockSpec((None, 1, LANES), lambda b, j: (layer, 0, j)),
                  pl.BlockSpec(memory_space=pl.ANY)],
        out_specs=pl.BlockSpec((L, LANES), lambda b, j: (b + blk0, out0 // LANES + j)),
        out_shape=jax.ShapeDtypeStruct((T, D), BF16),
        scratch_shapes=[pltpu.VMEM((L + 2 * MARGIN, LANES), F32), pltpu.VMEM((L, LANES), BF16)],
        input_output_aliases={3: 0},
        name="mix_pool",
        **_mixer_params(nseq, ng, 5 * L * LANES * 4),
    )(proj, pool_w, pool_scale, y)


def kernel(x_prompt, x_sample, meta_tokens, ln_in_g, ln_in_b, ffn1_w_gate, ffn1_w_up, ffn1_w_down, ln1_g, ln1_b, w_in, conv_w, conv_b, lru_conv_w, lru_conv_b, lru_w_a, lru_b_a, lru_w_x, lru_b_x, lru_lambda, pool_w, pool_scale, w_out, ln2_g, ln2_b, ffn2_w_gate, ffn2_w_up, ffn2_w_down, ln3_g, ln3_b):
    depth = ffn1_w_gate.shape[0]
    alpha = float((2 * depth) ** 0.25)
    bp, sp, D = x_prompt.shape
    bs, ss, _ = x_sample.shape
    lp, ls = sp + N_META, ss + N_META
    dconv = conv_w.shape[-1]
    dlru = lru_lambda.shape[-1]
    dpool = pool_scale.shape[-1]
    nheads = lru_w_a.shape[2]
    assert D == dconv + dlru + dpool and dlru // nheads == LANES and dpool // len(POOL_WINDOWS) == LANES
    assert dconv % LANES == 0 and sp % SEQ_TILE == 0 and ss % SEQ_TILE == 0 and ROW_TILE % PROJ_ROWS == 0
    assert lp % SCAN_SEGMENTS == 0 and ls % SCAN_SEGMENTS == 0
    assert max(POOL_WINDOWS) // 2 <= min(N_META, MARGIN)

    ts = bs * ls
    blk0_p = -(-ts // lp)
    off_p = blk0_p * lp
    t_real = off_p + bp * lp
    t_pad = -(-t_real // ROW_TILE) * ROW_TILE
    groups = ((x_sample, ls, bs, 0, ss), (x_prompt, lp, bp, blk0_p, sp))

    bf = lambda w: w.astype(BF16)
    row3 = lambda p: p.reshape(depth, 1, -1)
    f1 = (bf(ffn1_w_gate), bf(ffn1_w_up), bf(ffn1_w_down), row3(ln1_g), row3(ln1_b))
    f2 = (bf(ffn2_w_gate), bf(ffn2_w_up), bf(ffn2_w_down), row3(ln3_g), row3(ln3_b))
    w_in_b, w_out_b = bf(w_in), bf(w_out)
    gate_w = bf(0.5 * jnp.concatenate([lru_w_a, lru_w_x], axis=-1)).transpose(0, 2, 1, 3, 4)
    gate_b = 0.5 * jnp.concatenate([lru_b_a.reshape(depth, 2, nheads, LANES),
                                    lru_b_x.reshape(depth, 2, nheads, LANES)], axis=-1).transpose(0, 2, 1, 3)
    pool_w_b = bf(pool_w)
    conv_b3, pool_s3 = row3(conv_b), row3(pool_scale)
    ln2 = (row3(ln2_g), row3(ln2_b))
    lru_col, pool_col = 3 * dconv, 3 * dconv + 2 * dlru

    h = jnp.zeros((t_pad, D), F32)
    y = jnp.zeros((t_pad, D), BF16)
    for x, L, nseq, blk0, S in groups:
        h = _ln_in_call(x, h, ln_in_g, ln_in_b, L, blk0, SEQ_TILE, 0)
        meta = jnp.broadcast_to(meta_tokens[None], (nseq, N_META, D))
        h = _ln_in_call(meta, h, ln_in_g, ln_in_b, L, blk0, N_META, S)

    outs = []
    for layer in range(depth):
        h = _ffn_call(h, *f1, layer, alpha)
        proj = _inproj_call(h, w_in_b, layer)
        for _, L, nseq, blk0, _ in groups:
            y = _conv_call(proj, y, conv_w, conv_b3, layer, L, nseq, blk0, dconv)
            y = _lru_call(proj, y, lru_conv_w, lru_conv_b, gate_w, gate_b, lru_lambda,
                          layer, L, nseq, blk0, lru_col, dconv, nheads)
            y = _pool_call(proj, y, pool_w_b, pool_s3, layer, L, nseq, blk0, pool_col, dconv + dlru)
        h = _outproj_call(h, y, w_out_b, *ln2, layer, alpha)
        if layer + 1 < depth:
            h = _ffn_call(h, *f2, layer, alpha)
        else:
            outs = [_ffn_call(h, *f2, layer, alpha, seq=(L, nseq, blk0, S)).reshape(nseq, S, D)
                    for _, L, nseq, blk0, S in groups]

    y_sample, y_prompt = outs
    return (y_prompt, y_sample)
```

```python
import functools

import jax
import jax.numpy as jnp
from jax import lax
from jax.experimental import pallas as pl
from jax.experimental.pallas import tpu as pltpu

F32 = jnp.float32
BF16 = jnp.bfloat16

N_META = 16
POOL_WINDOWS = (2, 4, 8, 16)
SHORT_CONV_WIDTH = 3
LRU_CONV_WIDTH = 4
LRU_C = 8.0
LN_EPS = 1e-5
LOG2_E = 1.4426950408889634
GELU_C0 = 0.7978845608028654
GELU_C1 = 0.044715

LANES = 128
SUBLANES = 8
VMEM_BYTES_V7X = 64 * 1024 * 1024

ROW_TILE = 512
PROJ_ROWS = 512
SEQ_TILE = 512
FF_TILE = 512
PROJ_TILE = 1024
MIX_CHUNK = 1024
COPY_CHUNK = 256
MARGIN = SUBLANES
SCAN_SEGMENTS = 2 * SUBLANES


def _vmem_limit(block_bytes):
    return int(min(block_bytes + block_bytes // 2 + (8 << 20), VMEM_BYTES_V7X - (6 << 20)))


def _layernorm(z, g, b):
    mu = jnp.mean(z, axis=-1, keepdims=True)
    d = z - mu
    var = jnp.mean(d * d, axis=-1, keepdims=True)
    return d * lax.rsqrt(var + LN_EPS) * g + b


def _ln_kernel(x_ref, g_ref, b_ref, hprev_ref, o_ref):
    del hprev_ref
    o_ref[...] = _layernorm(x_ref[...], g_ref[...], b_ref[...])


def _ln_in_call(x, h, g, b, L, blk0, rows_per_step, row0):
    nseq, S, D = x.shape
    T = h.shape[0]
    nj = S // rows_per_step
    return pl.pallas_call(
        _ln_kernel,
        grid=(nseq, nj),
        in_specs=[pl.BlockSpec((None, rows_per_step, D), lambda s, j: (s, j, 0)),
                  pl.BlockSpec((1, D), lambda s, j: (0, 0)),
                  pl.BlockSpec((1, D), lambda s, j: (0, 0)),
                  pl.BlockSpec(memory_space=pl.ANY)],
        out_specs=pl.BlockSpec((pl.Element(rows_per_step), pl.Element(D)),
                               lambda s, j: (pl.multiple_of((blk0 + s) * L + row0 + j * rows_per_step,
                                                            SUBLANES), 0)),
        out_shape=jax.ShapeDtypeStruct((T, D), F32),
        input_output_aliases={3: 0},
        compiler_params=pltpu.CompilerParams(
            dimension_semantics=("arbitrary", "arbitrary"),
            vmem_limit_bytes=_vmem_limit(4 * rows_per_step * D * 4)),
        name="ln_in",
    )(x, g.reshape(1, D), b.reshape(1, D), h)


def _ffn_kernel(x_ref, wg_hbm, wu_hbm, wd_hbm, g_ref, b_ref, o_ref,
                wg_buf, wu_buf, wd_buf, sem, xb_ref, acc_ref, z_ref, *, nk, n_tiles, layer, alpha):
    i = pl.program_id(0)
    real = i < n_tiles
    ln_steps = 1 << ((nk - 2).bit_length() - 1)
    row_tile = x_ref.shape[0]
    ln_rows = row_tile // ln_steps

    def chunk_copies(c, slot):
        cols = pl.ds(c * FF_TILE, FF_TILE)
        return (pltpu.make_async_copy(wg_hbm.at[layer, :, cols], wg_buf.at[slot], sem.at[0, slot]),
                pltpu.make_async_copy(wu_hbm.at[layer, :, cols], wu_buf.at[slot], sem.at[1, slot]),
                pltpu.make_async_copy(wd_hbm.at[layer, cols, :], wd_buf.at[slot], sem.at[2, slot]))

    def normalise(rows):
        o_ref[rows] = _layernorm(z_ref[rows], g_ref[...], b_ref[...])

    @pl.when(i == 0)
    def _():
        z_ref[...] = jnp.zeros_like(z_ref)
        for copy in chunk_copies(0, 0):
            copy.start()

    @pl.when(real)
    def _():
        first = (i & 1) if nk % 2 else 0
        xb_ref[...] = x_ref[...].astype(BF16)
        for c in range(nk):
            slot = first ^ (c & 1)
            if c + 1 < nk:
                for copy in chunk_copies(c + 1, 1 - slot):
                    copy.start()
            else:
                @pl.when(i + 1 < n_tiles)
                def _():
                    for copy in chunk_copies(0, 1 - slot):
                        copy.start()
            for copy in chunk_copies(c, slot):
                copy.wait()
            xb = xb_ref[...]
            gate = jnp.dot(xb, wg_buf[slot], preferred_element_type=F32)
            up = jnp.dot(xb, wu_buf[slot], preferred_element_type=F32)
            hidden = (gate * jax.nn.sigmoid(gate) * up).astype(BF16)
            part = jnp.dot(hidden, wd_buf[slot], preferred_element_type=F32)
            if 1 <= c <= ln_steps:
                normalise(pl.ds((c - 1) * ln_rows, ln_rows))
            if c == 0:
                acc_ref[...] = part
            elif c + 1 < nk:
                acc_ref[...] += part
            else:
                z_ref[...] = alpha * x_ref[...] + 0.5 * (acc_ref[...] + part)

    @pl.when(jnp.logical_not(real))
    def _():
        normalise(pl.ds(0, row_tile))


def _ffn_call(h, wg, wu, wd, g, b, layer, alpha, seq=None):
    T, D = h.shape
    F = wg.shape[-1]
    nk = F // FF_TILE
    assert nk >= 3
    if seq is None:
        rt = ROW_TILE
        n_tiles, out_rows = T // rt, T
        x_spec = pl.BlockSpec((rt, D), lambda i: (jnp.minimum(i, n_tiles - 1), 0))
    else:
        L, nseq, blk0, S = seq
        rt = SEQ_TILE
        nj = S // rt
        n_tiles, out_rows = nseq * nj, nseq * S

        def x_rows(i):
            t = jnp.minimum(i, n_tiles - 1)
            return (pl.multiple_of((blk0 + t // nj) * L + (t % nj) * rt, SUBLANES), 0)

        x_spec = pl.BlockSpec((pl.Element(rt), pl.Element(D)), x_rows)
    blocks = 4 * rt * D * 4 + 6 * D * FF_TILE * 2 + rt * D * (2 + 4 + 4) + 3 * rt * FF_TILE * 4
    return pl.pallas_call(
        functools.partial(_ffn_kernel, nk=nk, n_tiles=n_tiles, layer=layer, alpha=alpha),
        grid=(n_tiles + 1,),
        in_specs=[x_spec,
                  pl.BlockSpec(memory_space=pl.ANY),
                  pl.BlockSpec(memory_space=pl.ANY),
                  pl.BlockSpec(memory_space=pl.ANY),
                  pl.BlockSpec((None, 1, D), lambda i: (layer, 0, 0)),
                  pl.BlockSpec((None, 1, D), lambda i: (layer, 0, 0))],
        out_specs=pl.BlockSpec((rt, D), lambda i: (jnp.maximum(i - 1, 0), 0)),
        out_shape=jax.ShapeDtypeStruct((out_rows, D), F32),
        scratch_shapes=[pltpu.VMEM((2, D, FF_TILE), BF16), pltpu.VMEM((2, D, FF_TILE), BF16),
                        pltpu.VMEM((2, FF_TILE, D), BF16), pltpu.SemaphoreType.DMA((3, 2)),
                        pltpu.VMEM((rt, D), BF16), pltpu.VMEM((rt, D), F32),
                        pltpu.VMEM((rt, D), F32)],
        compiler_params=pltpu.CompilerParams(
            dimension_semantics=("arbitrary",),
            vmem_limit_bytes=_vmem_limit(blocks)),
        name="ffn",
    )(h, wg, wu, wd, g, b)


def _inproj_kernel(x_ref, w_ref, o_ref):
    xb = x_ref[...].astype(BF16)
    for n in range(o_ref.shape[-1] // PROJ_TILE):
        cols = slice(n * PROJ_TILE, (n + 1) * PROJ_TILE)
        o_ref[:, cols] = jnp.dot(xb, w_ref[:, cols], preferred_element_type=F32)


def _inproj_call(h, w_in, layer):
    T, D = h.shape
    P = w_in.shape[-1]
    blocks = 2 * PROJ_ROWS * D * 4 + D * P * 2 + 2 * PROJ_ROWS * P * 4 + PROJ_ROWS * D * 2
    return pl.pallas_call(
        _inproj_kernel,
        grid=(T // PROJ_ROWS,),
        in_specs=[pl.BlockSpec((PROJ_ROWS, D), lambda i: (i, 0)),
                  pl.BlockSpec((None, D, P), lambda i: (layer, 0, 0), pipeline_mode=pl.Buffered(1))],
        out_specs=pl.BlockSpec((PROJ_ROWS, P), lambda i: (i, 0)),
        out_shape=jax.ShapeDtypeStruct((T, P), F32),
        compiler_params=pltpu.CompilerParams(
            dimension_semantics=("parallel",),
            vmem_limit_bytes=_vmem_limit(blocks)),
        name="in_proj",
    )(h, w_in)


def _outproj_kernel(h_ref, y_ref, w_ref, g_ref, b_ref, o_ref, *, alpha):
    mix = jnp.dot(y_ref[...], w_ref[...], preferred_element_type=F32)
    o_ref[...] = _layernorm(alpha * h_ref[...] + mix, g_ref[...], b_ref[...])


def _outproj_call(h, y, w_out, g, b, layer, alpha):
    T, D = h.shape
    rt = PROJ_ROWS
    blocks = 4 * rt * D * 4 + 2 * rt * D * 2 + D * D * 2 + 2 * rt * D * 4
    return pl.pallas_call(
        functools.partial(_outproj_kernel, alpha=alpha),
        grid=(T // rt,),
        in_specs=[pl.BlockSpec((rt, D), lambda i: (i, 0)),
                  pl.BlockSpec((rt, D), lambda i: (i, 0)),
                  pl.BlockSpec((None, D, D), lambda i: (layer, 0, 0), pipeline_mode=pl.Buffered(1)),
                  pl.BlockSpec((None, 1, D), lambda i: (layer, 0, 0)),
                  pl.BlockSpec((None, 1, D), lambda i: (layer, 0, 0))],
        out_specs=pl.BlockSpec((rt, D), lambda i: (i, 0)),
        out_shape=jax.ShapeDtypeStruct((T, D), F32),
        compiler_params=pltpu.CompilerParams(
            dimension_semantics=("parallel",),
            vmem_limit_bytes=_vmem_limit(blocks)),
        name="out_proj",
    )(h, y, w_out, g, b)


def _fill_padded(dst_ref, L, read):
    lx = L - N_META
    zeros = jnp.zeros((MARGIN, LANES), F32)
    dst_ref[0:MARGIN] = zeros
    dst_ref[MARGIN + L:2 * MARGIN + L] = zeros
    dst_ref[MARGIN:MARGIN + N_META] = read(pl.ds(lx, N_META))

    def body(i, c):
        r = pl.multiple_of(i * COPY_CHUNK, COPY_CHUNK)
        dst = pl.multiple_of(r + (MARGIN + N_META), SUBLANES)
        dst_ref[pl.ds(dst, COPY_CHUNK)] = read(pl.ds(r, COPY_CHUNK))
        return c

    lax.fori_loop(0, lx // COPY_CHUNK, body, 0)


def _for_storage_chunks(L, fn):
    lx = L - N_META
    last = lx // COPY_CHUNK - 1

    def body(i, c):
        r = pl.multiple_of(i * COPY_CHUNK, COPY_CHUNK)
        fn(pl.ds(r, COPY_CHUNK), r + N_META, COPY_CHUNK, False)
        return c

    lax.fori_loop(0, last, body, 0)
    fn(pl.ds(last * COPY_CHUNK, COPY_CHUNK), last * COPY_CHUNK + N_META, COPY_CHUNK, True)
    fn(pl.ds(lx, N_META), 0, N_META, True)


def _for_logical_chunks(L, chunk, fn):
    n_main = L // chunk

    def body(i, c):
        fn(pl.multiple_of(i * chunk, chunk), chunk)
        return c

    lax.fori_loop(0, n_main, body, 0)
    if L % chunk:
        fn(n_main * chunk, L % chunk)


def _conv_kernel(bg_ref, cg_ref, v_ref, w_ref, cb_ref, yprev_ref, o_ref, upad_ref, *, L):
    del yprev_ref
    _fill_padded(upad_ref, L, lambda rows: cg_ref[rows] * v_ref[rows])
    w = w_ref[...]
    cb = cb_ref[...]

    def emit(rows, t0, n, at_end):
        del at_end
        acc = cb
        for k in range(SHORT_CONV_WIDTH):
            off = k - SHORT_CONV_WIDTH // 2
            acc = acc + w[k:k + 1] * upad_ref[pl.ds(t0 + (MARGIN + off), n)]
        o_ref[rows] = (bg_ref[rows] * acc).astype(BF16)

    _for_storage_chunks(L, emit)


def _pool_kernel(u_ref, pw_ref, ps_ref, yprev_ref, o_ref, upad_ref, pooled_ref, *, L):
    del yprev_ref
    _fill_padded(upad_ref, L, lambda rows: u_ref[rows])
    group = pl.program_id(1)

    for gi, win in enumerate(POOL_WINDOWS):
        half = win // 2

        @pl.when(group == gi)
        def _(win=win, half=half):
            def emit(rows, t0, n, at_end):
                total = upad_ref[pl.ds(t0 + (MARGIN - half), n)]
                for j in range(1 - half, half):
                    total = total + upad_ref[pl.ds(t0 + (MARGIN + j), n)]
                if at_end:
                    t = t0 + lax.broadcasted_iota(jnp.int32, (n, LANES), 0)
                    lo = jnp.maximum(t - half, 0)
                    hi = jnp.minimum(t + (half - 1), L - 1)
                    mean = total / (hi - lo + 1).astype(F32)
                else:
                    mean = total * (1.0 / win)
                pooled_ref[rows] = (mean - upad_ref[pl.ds(t0 + MARGIN, n)]).astype(BF16)

            _for_storage_chunks(L, emit)

    mixed = jnp.dot(pooled_ref[...], pw_ref[...], preferred_element_type=F32)
    o_ref[...] = (mixed * ps_ref[...]).astype(BF16)


def _lru_kernel(x_ref, gate_ref, cw_ref, cb_ref, wg_ref, bg_ref, lam_ref, yprev_ref, o_ref,
                xpad_ref, xc_ref, a_ref, b_ref, h_ref, pre0_ref, pre1_ref, carry_ref, *, L):
    del yprev_ref
    seg = L // SCAN_SEGMENTS
    _fill_padded(xpad_ref, L, lambda rows: x_ref[rows])

    cw = cw_ref[...]
    cb = cb_ref[...]
    bg = bg_ref[...]
    neg_lam = -lam_ref[...]
    softplus = jnp.maximum(neg_lam, 0.0) + jnp.log1p(jnp.exp(-jnp.abs(neg_lam)))
    half_c2 = (-0.5 * LRU_C * LOG2_E) * softplus

    def conv(t0, n):
        for d in range(2):
            xc = cb[d:d + 1]
            for k in range(LRU_CONV_WIDTH):
                off = (k - (LRU_CONV_WIDTH - 1)) if d == 0 else ((LRU_CONV_WIDTH - 1) - k)
                xc = xc + cw[d, k:k + 1] * xpad_ref[pl.ds(t0 + (MARGIN + off), n)]
            xc_ref[d, pl.ds(t0, n)] = xc

    _for_logical_chunks(L, MIX_CHUNK, conv)

    def gate_matmuls(t0, n, pre_ref):
        for d in range(2):
            pre_ref[d, 0:n] = jnp.dot(xc_ref[d, pl.ds(t0, n)].astype(BF16), wg_ref[d],
                                      preferred_element_type=F32)

    def finish(t0, n, pre_ref):
        for d in range(2):
            pre = pre_ref[d, 0:n] + bg[d:d + 1]
            t_r = jnp.tanh(pre[:, :LANES])
            t_i = jnp.tanh(pre[:, LANES:])
            hc = half_c2[d:d + 1]
            a = jnp.exp2(hc + hc * t_r)
            gated = (0.5 * xc_ref[d, pl.ds(t0, n)]) * (1.0 + t_i)
            a_ref[d, pl.ds(t0, n)] = a
            v = 1.0 - a * a
            b_ref[d, pl.ds(t0, n)] = jnp.where(v > 0.0, v * lax.rsqrt(v), 0.0) * gated

    n_main = L // MIX_CHUNK
    assert n_main % 2 == 0
    gate_matmuls(0, MIX_CHUNK, pre0_ref)

    def gates_body(i, c):
        c0 = pl.multiple_of(2 * i * MIX_CHUNK, MIX_CHUNK)
        c1 = pl.multiple_of(c0 + MIX_CHUNK, MIX_CHUNK)
        nxt = pl.multiple_of(jnp.minimum(2 * i + 2, n_main - 1) * MIX_CHUNK, MIX_CHUNK)
        gate_matmuls(c1, MIX_CHUNK, pre1_ref)
        finish(c0, MIX_CHUNK, pre0_ref)
        gate_matmuls(nxt, MIX_CHUNK, pre0_ref)
        finish(c1, MIX_CHUNK, pre1_ref)
        return c

    lax.fori_loop(0, n_main // 2, gates_body, 0)
    if L % MIX_CHUNK:
        t0, n = n_main * MIX_CHUNK, L % MIX_CHUNK
        gate_matmuls(t0, n, pre0_ref)
        finish(t0, n, pre0_ref)

    nv = SCAN_SEGMENTS // SUBLANES

    def seg_rows(j, t):
        return pl.ds(j * SUBLANES * seg + t, SUBLANES, stride=seg)

    def ends_body(t, c):
        out = []
        for d in range(2):
            tt = t if d == 0 else seg - 1 - t
            for j in range(nv):
                h, p = c[2 * (d * nv + j)], c[2 * (d * nv + j) + 1]
                rows = seg_rows(j, tt)
                a = a_ref[d, rows]
                out += [a * h + b_ref[d, rows], p * a]
        return tuple(out)

    zero = jnp.zeros((SUBLANES, LANES), F32)
    one = jnp.ones((SUBLANES, LANES), F32)
    ends = lax.fori_loop(0, seg, ends_body, (zero, one) * (2 * nv), unroll=4)

    S = SCAN_SEGMENTS
    for d in range(2):
        for j in range(nv):
            carry_ref[d, pl.ds(j * SUBLANES, SUBLANES)] = ends[2 * (d * nv + j)]
            carry_ref[d, pl.ds(S + j * SUBLANES, SUBLANES)] = ends[2 * (d * nv + j) + 1]
        state = jnp.zeros((1, LANES), F32)
        for s in (range(S) if d == 0 else reversed(range(S))):
            carry_ref[d, 2 * S + s:2 * S + s + 1] = state
            state = carry_ref[d, S + s:S + s + 1] * state + carry_ref[d, s:s + 1]
    cin = tuple(carry_ref[d, pl.ds(2 * S + j * SUBLANES, SUBLANES)] for d in range(2) for j in range(nv))

    def scan_body(t, c):
        out = []
        for d in range(2):
            tt = t if d == 0 else seg - 1 - t
            for j in range(nv):
                rows = seg_rows(j, tt)
                h = a_ref[d, rows] * c[d * nv + j] + b_ref[d, rows]
                h_ref[d, rows] = h
                out.append(h)
        return tuple(out)

    lax.fori_loop(0, seg, scan_body, cin, unroll=4)

    def emit(rows, t0, n, at_end):
        del at_end
        g = gate_ref[rows]
        t = jnp.tanh(g * (GELU_C0 + (GELU_C0 * GELU_C1) * (g * g)))
        hsum = h_ref[0, pl.ds(t0, n)] + h_ref[1, pl.ds(t0, n)]
        o_ref[rows] = (((0.5 * g) * (1.0 + t)) * hsum).astype(BF16)

    _for_storage_chunks(L, emit)


def _mixer_params(nseq, nblk, scratch_bytes):
    return dict(
        grid=(nseq, nblk),
        compiler_params=pltpu.CompilerParams(
            dimension_semantics=("parallel", "arbitrary"),
            vmem_limit_bytes=_vmem_limit(scratch_bytes)),
    )


def _conv_call(proj, y, conv_w, conv_b, layer, L, nseq, blk0, dconv):
    nb = dconv // LANES
    T, D = y.shape
    seq = lambda base: pl.BlockSpec((L, LANES), lambda b, j: (b + blk0, base + j))
    return pl.pallas_call(
        functools.partial(_conv_kernel, L=L),
        in_specs=[seq(0), seq(nb), seq(2 * nb),
                  pl.BlockSpec((None, SHORT_CONV_WIDTH, LANES), lambda b, j: (layer, 0, j)),
                  pl.BlockSpec((None, 1, LANES), lambda b, j: (layer, 0, j)),
                  pl.BlockSpec(memory_space=pl.ANY)],
        out_specs=pl.BlockSpec((L, LANES), lambda b, j: (b + blk0, j)),
        out_shape=jax.ShapeDtypeStruct((T, D), BF16),
        scratch_shapes=[pltpu.VMEM((L + 2 * MARGIN, LANES), F32)],
        input_output_aliases={5: 0},
        name="mix_conv",
        **_mixer_params(nseq, nb, 9 * L * LANES * 4),
    )(proj, proj, proj, conv_w, conv_b, y)


def _lru_call(proj, y, cw, cb, wg, bg, lam, layer, L, nseq, blk0, col0, out0, nheads):
    T, D = y.shape
    c0 = col0 // LANES
    return pl.pallas_call(
        functools.partial(_lru_kernel, L=L),
        in_specs=[pl.BlockSpec((L, LANES), lambda b, j: (b + blk0, c0 + j)),
                  pl.BlockSpec((L, LANES), lambda b, j: (b + blk0, c0 + nheads + j)),
                  pl.BlockSpec((None, 2, LRU_CONV_WIDTH, LANES), lambda b, j: (layer, 0, 0, j)),
                  pl.BlockSpec((None, 2, LANES), lambda b, j: (layer, 0, j)),
                  pl.BlockSpec((None, None, 2, LANES, 2 * LANES), lambda b, j: (layer, j, 0, 0, 0)),
                  pl.BlockSpec((None, None, 2, 2 * LANES), lambda b, j: (layer, j, 0, 0)),
                  pl.BlockSpec((None, 2, LANES), lambda b, j: (layer, 0, j)),
                  pl.BlockSpec(memory_space=pl.ANY)],
        out_specs=pl.BlockSpec((L, LANES), lambda b, j: (b + blk0, out0 // LANES + j)),
        out_shape=jax.ShapeDtypeStruct((T, D), BF16),
        scratch_shapes=[pltpu.VMEM((L + 2 * MARGIN, LANES), F32)]
        + [pltpu.VMEM((2, L, LANES), F32)] * 4
        + [pltpu.VMEM((2, MIX_CHUNK, 2 * LANES), F32)] * 2
        + [pltpu.VMEM((2, 3 * SCAN_SEGMENTS, LANES), F32)],
        input_output_aliases={7: 0},
        name="mix_lru",
        **_mixer_params(nseq, nheads, 15 * L * LANES * 4),
    )(proj, proj, cw, cb, wg, bg, lam, y)


def _pool_call(proj, y, pool_w, pool_scale, layer, L, nseq, blk0, col0, out0):
    T, D = y.shape
    ng = len(POOL_WINDOWS)
    return pl.pallas_call(
        functools.partial(_pool_kernel, L=L),
        in_specs=[pl.BlockSpec((L, LANES), lambda b, j: (b + blk0, col0 // LANES + j)),
                  pl.BlockSpec((None, None, LANES, LANES), lambda b, j: (layer, j, 0, 0)),
                  pl.BlockSpec((None, 1, LANES), lambda b, j: (layer, 0, j)),
                  pl.BlockSpec(memory_space=pl.ANY)],
        out_specs=pl.BlockSpec((L, LANES), lambda b, j: (b + blk0, out0 // LANES + j)),
        out_shape=jax.ShapeDtypeStruct((T, D), BF16),
        scratch_shapes=[pltpu.VMEM((L + 2 * MARGIN, LANES), F32), pltpu.VMEM((L, LANES), BF16)],
        input_output_aliases={3: 0},
        name="mix_pool",
        **_mixer_params(nseq, ng, 5 * L * LANES * 4),
    )(proj, pool_w, pool_scale, y)


def kernel(x_prompt, x_sample, meta_tokens, ln_in_g, ln_in_b, ffn1_w_gate, ffn1_w_up, ffn1_w_down, ln1_g, ln1_b, w_in, conv_w, conv_b, lru_conv_w, lru_conv_b, lru_w_a, lru_b_a, lru_w_x, lru_b_x, lru_lambda, pool_w, pool_scale, w_out, ln2_g, ln2_b, ffn2_w_gate, ffn2_w_up, ffn2_w_down, ln3_g, ln3_b):
    depth = ffn1_w_gate.shape[0]
    alpha = float((2 * depth) ** 0.25)
    bp, sp, D = x_prompt.shape
    bs, ss, _ = x_sample.shape
    lp, ls = sp + N_META, ss + N_META
    dconv = conv_w.shape[-1]
    dlru = lru_lambda.shape[-1]
    dpool = pool_scale.shape[-1]
    nheads = lru_w_a.shape[2]
    assert D == dconv + dlru + dpool and dlru // nheads == LANES and dpool // len(POOL_WINDOWS) == LANES
    assert dconv % LANES == 0 and sp % SEQ_TILE == 0 and ss % SEQ_TILE == 0 and ROW_TILE % PROJ_ROWS == 0
    assert lp % SCAN_SEGMENTS == 0 and ls % SCAN_SEGMENTS == 0
    assert max(POOL_WINDOWS) // 2 <= min(N_META, MARGIN)

    ts = bs * ls
    blk0_p = -(-ts // lp)
    off_p = blk0_p * lp
    t_real = off_p + bp * lp
    t_pad = -(-t_real // ROW_TILE) * ROW_TILE
    groups = ((x_sample, ls, bs, 0, ss), (x_prompt, lp, bp, blk0_p, sp))

    bf = lambda w: w.astype(BF16)
    row3 = lambda p: p.reshape(depth, 1, -1)
    f1 = (bf(ffn1_w_gate), bf(ffn1_w_up), bf(ffn1_w_down), row3(ln1_g), row3(ln1_b))
    f2 = (bf(ffn2_w_gate), bf(ffn2_w_up), bf(ffn2_w_down), row3(ln3_g), row3(ln3_b))
    w_in_b, w_out_b = bf(w_in), bf(w_out)
    gate_w = bf(0.5 * jnp.concatenate([lru_w_a, lru_w_x], axis=-1)).transpose(0, 2, 1, 3, 4)
    gate_b = 0.5 * jnp.concatenate([lru_b_a.reshape(depth, 2, nheads, LANES),
                                    lru_b_x.reshape(depth, 2, nheads, LANES)], axis=-1).transpose(0, 2, 1, 3)
    pool_w_b = bf(pool_w)
    conv_b3, pool_s3 = row3(conv_b), row3(pool_scale)
    ln2 = (row3(ln2_g), row3(ln2_b))
    lru_col, pool_col = 3 * dconv, 3 * dconv + 2 * dlru

    h = jnp.zeros((t_pad, D), F32)
    y = jnp.zeros((t_pad, D), BF16)
    for x, L, nseq, blk0, S in groups:
        h = _ln_in_call(x, h, ln_in_g, ln_in_b, L, blk0, SEQ_TILE, 0)
        meta = jnp.broadcast_to(meta_tokens[None], (nseq, N_META, D))
        h = _ln_in_call(meta, h, ln_in_g, ln_in_b, L, blk0, N_META, S)

    outs = []
    for layer in range(depth):
        h = _ffn_call(h, *f1, layer, alpha)
        proj = _inproj_call(h, w_in_b, layer)
        for _, L, nseq, blk0, _ in groups:
            y = _conv_call(proj, y, conv_w, conv_b3, layer, L, nseq, blk0, dconv)
            y = _lru_call(proj, y, lru_conv_w, lru_conv_b, gate_w, gate_b, lru_lambda,
                          layer, L, nseq, blk0, lru_col, dconv, nheads)
            y = _pool_call(proj, y, pool_w_b, pool_s3, layer, L, nseq, blk0, pool_col, dconv + dlru)
        h = _outproj_call(h, y, w_out_b, *ln2, layer, alpha)
        if layer + 1 < depth:
            h = _ffn_call(h, *f2, layer, alpha)
        else:
            outs = [_ffn_call(h, *f2, layer, alpha, seq=(L, nseq, blk0, S)).reshape(nseq, S, D)
                    for _, L, nseq, blk0, S in groups]

    y_sample, y_prompt = outs
    return (y_prompt, y_sample)
```

```python
import functools

import jax
import jax.numpy as jnp
from jax import lax
from jax.experimental import pallas as pl
from jax.experimental.pallas import tpu as pltpu

F32 = jnp.float32
BF16 = jnp.bfloat16

N_META = 16
POOL_WINDOWS = (2, 4, 8, 16)
SHORT_CONV_WIDTH = 3
LRU_CONV_WIDTH = 4
LRU_C = 8.0
LN_EPS = 1e-5
LOG2_E = 1.4426950408889634
GELU_C0 = 0.7978845608028654
GELU_C1 = 0.044715

LANES = 128
SUBLANES = 8
VMEM_BYTES_V7X = 64 * 1024 * 1024

ROW_TILE = 512
PROJ_ROWS = 512
SEQ_TILE = 512
FF_TILE = 512
PROJ_TILE = 1024
MIX_CHUNK = 1024
POOL_PACK = 2
COPY_CHUNK = 256
MARGIN = SUBLANES
SCAN_SEGMENTS = 2 * SUBLANES


def _vmem_limit(block_bytes):
    return int(min(block_bytes + block_bytes // 2 + (8 << 20), VMEM_BYTES_V7X - (6 << 20)))


def _layernorm(z, g, b):
    mu = jnp.mean(z, axis=-1, keepdims=True)
    d = z - mu
    var = jnp.mean(d * d, axis=-1, keepdims=True)
    return d * lax.rsqrt(var + LN_EPS) * g + b


def _ln_kernel(x_ref, g_ref, b_ref, hprev_ref, o_ref):
    del hprev_ref
    o_ref[...] = _layernorm(x_ref[...], g_ref[...], b_ref[...])


def _ln_in_call(x, h, g, b, L, blk0, rows_per_step, row0):
    nseq, S, D = x.shape
    T = h.shape[0]
    nj = S // rows_per_step
    return pl.pallas_call(
        _ln_kernel,
        grid=(nseq, nj),
        in_specs=[pl.BlockSpec((None, rows_per_step, D), lambda s, j: (s, j, 0)),
                  pl.BlockSpec((1, D), lambda s, j: (0, 0)),
                  pl.BlockSpec((1, D), lambda s, j: (0, 0)),
                  pl.BlockSpec(memory_space=pl.ANY)],
        out_specs=pl.BlockSpec((pl.Element(rows_per_step), pl.Element(D)),
                               lambda s, j: (pl.multiple_of((blk0 + s) * L + row0 + j * rows_per_step,
                                                            SUBLANES), 0)),
        out_shape=jax.ShapeDtypeStruct((T, D), F32),
        input_output_aliases={3: 0},
        compiler_params=pltpu.CompilerParams(
            dimension_semantics=("arbitrary", "arbitrary"),
            vmem_limit_bytes=_vmem_limit(4 * rows_per_step * D * 4)),
        name="ln_in",
    )(x, g.reshape(1, D), b.reshape(1, D), h)


def _ffn_kernel(x_ref, wg_hbm, wu_hbm, wd_hbm, g_ref, b_ref, o_ref,
                wg_buf, wu_buf, wd_buf, sem, xb_ref, acc_ref, z_ref, *, nk, n_tiles, layer, alpha):
    i = pl.program_id(0)
    real = i < n_tiles
    ln_steps = 1 << ((nk - 2).bit_length() - 1)
    row_tile = x_ref.shape[0]
    ln_rows = row_tile // ln_steps

    def chunk_copies(c, slot):
        cols = pl.ds(c * FF_TILE, FF_TILE)
        return (pltpu.make_async_copy(wg_hbm.at[layer, :, cols], wg_buf.at[slot], sem.at[0, slot]),
                pltpu.make_async_copy(wu_hbm.at[layer, :, cols], wu_buf.at[slot], sem.at[1, slot]),
                pltpu.make_async_copy(wd_hbm.at[layer, cols, :], wd_buf.at[slot], sem.at[2, slot]))

    def normalise(rows):
        o_ref[rows] = _layernorm(z_ref[rows], g_ref[...], b_ref[...])

    @pl.when(i == 0)
    def _():
        z_ref[...] = jnp.zeros_like(z_ref)
        for copy in chunk_copies(0, 0):
            copy.start()

    @pl.when(real)
    def _():
        first = (i & 1) if nk % 2 else 0
        xb_ref[...] = x_ref[...].astype(BF16)
        for c in range(nk):
            slot = first ^ (c & 1)
            if c + 1 < nk:
                for copy in chunk_copies(c + 1, 1 - slot):
                    copy.start()
            else:
                @pl.when(i + 1 < n_tiles)
                def _():
                    for copy in chunk_copies(0, 1 - slot):
                        copy.start()
            for copy in chunk_copies(c, slot):
                copy.wait()
            xb = xb_ref[...]
            gate = jnp.dot(xb, wg_buf[slot], preferred_element_type=F32)
            up = jnp.dot(xb, wu_buf[slot], preferred_element_type=F32)
            hidden = (gate * jax.nn.sigmoid(gate) * up).astype(BF16)
            part = jnp.dot(hidden, wd_buf[slot], preferred_element_type=F32)
            if 1 <= c <= ln_steps:
                normalise(pl.ds((c - 1) * ln_rows, ln_rows))
            if c == 0:
                acc_ref[...] = part
            elif c + 1 < nk:
                acc_ref[...] += part
            else:
                z_ref[...] = alpha * x_ref[...] + 0.5 * (acc_ref[...] + part)

    @pl.when(jnp.logical_not(real))
    def _():
        normalise(pl.ds(0, row_tile))


def _ffn_call(h, wg, wu, wd, g, b, layer, alpha, seq=None):
    T, D = h.shape
    F = wg.shape[-1]
    nk = F // FF_TILE
    assert nk >= 3
    if seq is None:
        rt = ROW_TILE
        n_tiles, out_rows = T // rt, T
        x_spec = pl.BlockSpec((rt, D), lambda i: (jnp.minimum(i, n_tiles - 1), 0))
    else:
        L, nseq, blk0, S = seq
        rt = SEQ_TILE
        nj = S // rt
        n_tiles, out_rows = nseq * nj, nseq * S

        def x_rows(i):
            t = jnp.minimum(i, n_tiles - 1)
            return (pl.multiple_of((blk0 + t // nj) * L + (t % nj) * rt, SUBLANES), 0)

        x_spec = pl.BlockSpec((pl.Element(rt), pl.Element(D)), x_rows)
    blocks = 4 * rt * D * 4 + 6 * D * FF_TILE * 2 + rt * D * (2 + 4 + 4) + 3 * rt * FF_TILE * 4
    return pl.pallas_call(
        functools.partial(_ffn_kernel, nk=nk, n_tiles=n_tiles, layer=layer, alpha=alpha),
        grid=(n_tiles + 1,),
        in_specs=[x_spec,
                  pl.BlockSpec(memory_space=pl.ANY),
                  pl.BlockSpec(memory_space=pl.ANY),
                  pl.BlockSpec(memory_space=pl.ANY),
                  pl.BlockSpec((None, 1, D), lambda i: (layer, 0, 0)),
                  pl.BlockSpec((None, 1, D), lambda i: (layer, 0, 0))],
        out_specs=pl.BlockSpec((rt, D), lambda i: (jnp.maximum(i - 1, 0), 0)),
        out_shape=jax.ShapeDtypeStruct((out_rows, D), F32),
        scratch_shapes=[pltpu.VMEM((2, D, FF_TILE), BF16), pltpu.VMEM((2, D, FF_TILE), BF16),
                        pltpu.VMEM((2, FF_TILE, D), BF16), pltpu.SemaphoreType.DMA((3, 2)),
                        pltpu.VMEM((rt, D), BF16), pltpu.VMEM((rt, D), F32),
                        pltpu.VMEM((rt, D), F32)],
        compiler_params=pltpu.CompilerParams(
            dimension_semantics=("arbitrary",),
            vmem_limit_bytes=_vmem_limit(blocks)),
        name="ffn",
    )(h, wg, wu, wd, g, b)


def _inproj_kernel(x_ref, w_ref, o_ref):
    xb = x_ref[...].astype(BF16)
    for n in range(o_ref.shape[-1] // PROJ_TILE):
        cols = slice(n * PROJ_TILE, (n + 1) * PROJ_TILE)
        o_ref[:, cols] = jnp.dot(xb, w_ref[:, cols], preferred_element_type=F32)


def _inproj_call(h, w_in, layer):
    T, D = h.shape
    P = w_in.shape[-1]
    blocks = 2 * PROJ_ROWS * D * 4 + D * P * 2 + 2 * PROJ_ROWS * P * 4 + PROJ_ROWS * D * 2
    return pl.pallas_call(
        _inproj_kernel,
        grid=(T // PROJ_ROWS,),
        in_specs=[pl.BlockSpec((PROJ_ROWS, D), lambda i: (i, 0)),
                  pl.BlockSpec((None, D, P), lambda i: (layer, 0, 0), pipeline_mode=pl.Buffered(1))],
        out_specs=pl.BlockSpec((PROJ_ROWS, P), lambda i: (i, 0)),
        out_shape=jax.ShapeDtypeStruct((T, P), F32),
        compiler_params=pltpu.CompilerParams(
            dimension_semantics=("parallel",),
            vmem_limit_bytes=_vmem_limit(blocks)),
        name="in_proj",
    )(h, w_in)


def _outproj_kernel(h_ref, y_ref, w_ref, g_ref, b_ref, o_ref, *, alpha):
    mix = jnp.dot(y_ref[...], w_ref[...], preferred_element_type=F32)
    o_ref[...] = _layernorm(alpha * h_ref[...] + mix, g_ref[...], b_ref[...])


def _outproj_call(h, y, w_out, g, b, layer, alpha):
    T, D = h.shape
    rt = PROJ_ROWS
    blocks = 4 * rt * D * 4 + 2 * rt * D * 2 + D * D * 2 + 2 * rt * D * 4
    return pl.pallas_call(
        functools.partial(_outproj_kernel, alpha=alpha),
        grid=(T // rt,),
        in_specs=[pl.BlockSpec((rt, D), lambda i: (i, 0)),
                  pl.BlockSpec((rt, D), lambda i: (i, 0)),
                  pl.BlockSpec((None, D, D), lambda i: (layer, 0, 0), pipeline_mode=pl.Buffered(1)),
                  pl.BlockSpec((None, 1, D), lambda i: (layer, 0, 0)),
                  pl.BlockSpec((None, 1, D), lambda i: (layer, 0, 0))],
        out_specs=pl.BlockSpec((rt, D), lambda i: (i, 0)),
        out_shape=jax.ShapeDtypeStruct((T, D), F32),
        compiler_params=pltpu.CompilerParams(
            dimension_semantics=("parallel",),
            vmem_limit_bytes=_vmem_limit(blocks)),
        name="out_proj",
    )(h, y, w_out, g, b)


def _fill_padded(dst_ref, L, read):
    lx = L - N_META
    zeros = jnp.zeros((MARGIN, LANES), F32)
    dst_ref[0:MARGIN] = zeros
    dst_ref[MARGIN + L:2 * MARGIN + L] = zeros
    dst_ref[MARGIN:MARGIN + N_META] = read(pl.ds(lx, N_META))

    def body(i, c):
        r = pl.multiple_of(i * COPY_CHUNK, COPY_CHUNK)
        dst = pl.multiple_of(r + (MARGIN + N_META), SUBLANES)
        dst_ref[pl.ds(dst, COPY_CHUNK)] = read(pl.ds(r, COPY_CHUNK))
        return c

    lax.fori_loop(0, lx // COPY_CHUNK, body, 0)


def _for_storage_chunks(L, fn):
    lx = L - N_META
    last = lx // COPY_CHUNK - 1

    def body(i, c):
        r = pl.multiple_of(i * COPY_CHUNK, COPY_CHUNK)
        fn(pl.ds(r, COPY_CHUNK), r + N_META, COPY_CHUNK, False)
        return c

    lax.fori_loop(0, last, body, 0)
    fn(pl.ds(last * COPY_CHUNK, COPY_CHUNK), last * COPY_CHUNK + N_META, COPY_CHUNK, True)
    fn(pl.ds(lx, N_META), 0, N_META, True)


def _for_logical_chunks(L, chunk, fn):
    n_main = L // chunk

    def body(i, c):
        fn(pl.multiple_of(i * chunk, chunk), chunk)
        return c

    lax.fori_loop(0, n_main, body, 0)
    if L % chunk:
        fn(n_main * chunk, L % chunk)


def _conv_kernel(bg_ref, cg_ref, v_ref, w_ref, cb_ref, yprev_ref, o_ref, upad_ref, *, L):
    del yprev_ref
    _fill_padded(upad_ref, L, lambda rows: cg_ref[rows] * v_ref[rows])
    w = w_ref[...]
    cb = cb_ref[...]

    def emit(rows, t0, n, at_end):
        del at_end
        acc = cb
        for k in range(SHORT_CONV_WIDTH):
            off = k - SHORT_CONV_WIDTH // 2
            acc = acc + w[k:k + 1] * upad_ref[pl.ds(t0 + (MARGIN + off), n)]
        o_ref[rows] = (bg_ref[rows] * acc).astype(BF16)

    _for_storage_chunks(L, emit)


def _pool_kernel(u_ref, pw_ref, ps_ref, yprev_ref, o_ref, upad_ref, pooled_ref, *, L):
    del yprev_ref
    pack = pl.program_id(1)

    for part in range(POOL_PACK):
        lanes = slice(part * LANES, (part + 1) * LANES)
        _fill_padded(upad_ref, L, lambda rows, lanes=lanes: u_ref[rows, lanes])

        for gi, win in enumerate(POOL_WINDOWS):
            if gi % POOL_PACK != part:
                continue
            half = win // 2

            @pl.when(pack == gi // POOL_PACK)
            def _(win=win, half=half, lanes=lanes):
                def emit(rows, t0, n, at_end):
                    total = upad_ref[pl.ds(t0 + (MARGIN - half), n)]
                    for j in range(1 - half, half):
                        total = total + upad_ref[pl.ds(t0 + (MARGIN + j), n)]
                    if at_end:
                        t = t0 + lax.broadcasted_iota(jnp.int32, (n, LANES), 0)
                        lo = jnp.maximum(t - half, 0)
                        hi = jnp.minimum(t + (half - 1), L - 1)
                        mean = total / (hi - lo + 1).astype(F32)
                    else:
                        mean = total * (1.0 / win)
                    pooled_ref[rows, lanes] = (mean - upad_ref[pl.ds(t0 + MARGIN, n)]).astype(BF16)

                _for_storage_chunks(L, emit)

    mixed = jnp.dot(pooled_ref[...], pw_ref[...], preferred_element_type=F32)
    o_ref[...] = (mixed * ps_ref[...]).astype(BF16)


def _lru_kernel(x_ref, gate_ref, cw_ref, cb_ref, wg_ref, bg_ref, lam_ref, yprev_ref, o_ref,
                xpad_ref, xc_ref, a_ref, b_ref, h_ref, pre0_ref, pre1_ref, carry_ref, *, L):
    del yprev_ref
    seg = L // SCAN_SEGMENTS
    _fill_padded(xpad_ref, L, lambda rows: x_ref[rows])

    cw = cw_ref[...]
    cb = cb_ref[...]
    bg = bg_ref[...]
    neg_lam = -lam_ref[...]
    softplus = jnp.maximum(neg_lam, 0.0) + jnp.log1p(jnp.exp(-jnp.abs(neg_lam)))
    half_c2 = (-0.5 * LRU_C * LOG2_E) * softplus

    def conv(t0, n):
        for d in range(2):
            xc = cb[d:d + 1]
            for k in range(LRU_CONV_WIDTH):
                off = (k - (LRU_CONV_WIDTH - 1)) if d == 0 else ((LRU_CONV_WIDTH - 1) - k)
                xc = xc + cw[d, k:k + 1] * xpad_ref[pl.ds(t0 + (MARGIN + off), n)]
            xc_ref[d, pl.ds(t0, n)] = xc

    _for_logical_chunks(L, MIX_CHUNK, conv)

    def gate_matmuls(t0, n, pre_ref):
        for d in range(2):
            pre_ref[d, 0:n] = jnp.dot(xc_ref[d, pl.ds(t0, n)].astype(BF16), wg_ref[d],
                                      preferred_element_type=F32)

    def finish(t0, n, pre_ref):
        for d in range(2):
            pre = pre_ref[d, 0:n] + bg[d:d + 1]
            t_r = jnp.tanh(pre[:, :LANES])
            t_i = jnp.tanh(pre[:, LANES:])
            hc = half_c2[d:d + 1]
            a = jnp.exp2(hc + hc * t_r)
            gated = (0.5 * xc_ref[d, pl.ds(t0, n)]) * (1.0 + t_i)
            a_ref[d, pl.ds(t0, n)] = a
            v = 1.0 - a * a
            b_ref[d, pl.ds(t0, n)] = jnp.where(v > 0.0, v * lax.rsqrt(v), 0.0) * gated

    n_main = L // MIX_CHUNK
    assert n_main % 2 == 0
    gate_matmuls(0, MIX_CHUNK, pre0_ref)

    def gates_body(i, c):
        c0 = pl.multiple_of(2 * i * MIX_CHUNK, MIX_CHUNK)
        c1 = pl.multiple_of(c0 + MIX_CHUNK, MIX_CHUNK)
        nxt = pl.multiple_of(jnp.minimum(2 * i + 2, n_main - 1) * MIX_CHUNK, MIX_CHUNK)
        gate_matmuls(c1, MIX_CHUNK, pre1_ref)
        finish(c0, MIX_CHUNK, pre0_ref)
        gate_matmuls(nxt, MIX_CHUNK, pre0_ref)
        finish(c1, MIX_CHUNK, pre1_ref)
        return c

    lax.fori_loop(0, n_main // 2, gates_body, 0)
    if L % MIX_CHUNK:
        t0, n = n_main * MIX_CHUNK, L % MIX_CHUNK
        gate_matmuls(t0, n, pre0_ref)
        finish(t0, n, pre0_ref)

    nv = SCAN_SEGMENTS // SUBLANES

    def seg_rows(j, t):
        return pl.ds(j * SUBLANES * seg + t, SUBLANES, stride=seg)

    def ends_body(t, c):
        out = []
        for d in range(2):
            tt = t if d == 0 else seg - 1 - t
            for j in range(nv):
                h, p = c[2 * (d * nv + j)], c[2 * (d * nv + j) + 1]
                rows = seg_rows(j, tt)
                a = a_ref[d, rows]
                out += [a * h + b_ref[d, rows], p * a]
        return tuple(out)

    zero = jnp.zeros((SUBLANES, LANES), F32)
    one = jnp.ones((SUBLANES, LANES), F32)
    ends = lax.fori_loop(0, seg, ends_body, (zero, one) * (2 * nv), unroll=4)

    S = SCAN_SEGMENTS
    for d in range(2):
        for j in range(nv):
            carry_ref[d, pl.ds(j * SUBLANES, SUBLANES)] = ends[2 * (d * nv + j)]
            carry_ref[d, pl.ds(S + j * SUBLANES, SUBLANES)] = ends[2 * (d * nv + j) + 1]
        state = jnp.zeros((1, LANES), F32)
        for s in (range(S) if d == 0 else reversed(range(S))):
            carry_ref[d, 2 * S + s:2 * S + s + 1] = state
            state = carry_ref[d, S + s:S + s + 1] * state + carry_ref[d, s:s + 1]
    cin = tuple(carry_ref[d, pl.ds(2 * S + j * SUBLANES, SUBLANES)] for d in range(2) for j in range(nv))

    def scan_body(t, c):
        out = []
        for d in range(2):
            tt = t if d == 0 else seg - 1 - t
            for j in range(nv):
                rows = seg_rows(j, tt)
                h = a_ref[d, rows] * c[d * nv + j] + b_ref[d, rows]
                h_ref[d, rows] = h
                out.append(h)
        return tuple(out)

    lax.fori_loop(0, seg, scan_body, cin, unroll=4)

    def emit(rows, t0, n, at_end):
        del at_end
        g = gate_ref[rows]
        t = jnp.tanh(g * (GELU_C0 + (GELU_C0 * GELU_C1) * (g * g)))
        hsum = h_ref[0, pl.ds(t0, n)] + h_ref[1, pl.ds(t0, n)]
        o_ref[rows] = (((0.5 * g) * (1.0 + t)) * hsum).astype(BF16)

    _for_storage_chunks(L, emit)


def _mixer_params(nseq, nblk, scratch_bytes):
    return dict(
        grid=(nseq, nblk),
        compiler_params=pltpu.CompilerParams(
            dimension_semantics=("parallel", "arbitrary"),
            vmem_limit_bytes=_vmem_limit(scratch_bytes)),
    )


def _conv_call(proj, y, conv_w, conv_b, layer, L, nseq, blk0, dconv):
    nb = dconv // LANES
    T, D = y.shape
    seq = lambda base: pl.BlockSpec((L, LANES), lambda b, j: (b + blk0, base + j))
    return pl.pallas_call(
        functools.partial(_conv_kernel, L=L),
        in_specs=[seq(0), seq(nb), seq(2 * nb),
                  pl.BlockSpec((None, SHORT_CONV_WIDTH, LANES), lambda b, j: (layer, 0, j)),
                  pl.BlockSpec((None, 1, LANES), lambda b, j: (layer, 0, j)),
                  pl.BlockSpec(memory_space=pl.ANY)],
        out_specs=pl.BlockSpec((L, LANES), lambda b, j: (b + blk0, j)),
        out_shape=jax.ShapeDtypeStruct((T, D), BF16),
        scratch_shapes=[pltpu.VMEM((L + 2 * MARGIN, LANES), F32)],
        input_output_aliases={5: 0},
        name="mix_conv",
        **_mixer_params(nseq, nb, 9 * L * LANES * 4),
    )(proj, proj, proj, conv_w, conv_b, y)


def _lru_call(proj, y, cw, cb, wg, bg, lam, layer, L, nseq, blk0, col0, out0, nheads):
    T, D = y.shape
    c0 = col0 // LANES
    return pl.pallas_call(
        functools.partial(_lru_kernel, L=L),
        in_specs=[pl.BlockSpec((L, LANES), lambda b, j: (b + blk0, c0 + j)),
                  pl.BlockSpec((L, LANES), lambda b, j: (b + blk0, c0 + nheads + j)),
                  pl.BlockSpec((None, 2, LRU_CONV_WIDTH, LANES), lambda b, j: (layer, 0, 0, j)),
                  pl.BlockSpec((None, 2, LANES), lambda b, j: (layer, 0, j)),
                  pl.BlockSpec((None, None, 2, LANES, 2 * LANES), lambda b, j: (layer, j, 0, 0, 0)),
                  pl.BlockSpec((None, None, 2, 2 * LANES), lambda b, j: (layer, j, 0, 0)),
                  pl.BlockSpec((None, 2, LANES), lambda b, j: (layer, 0, j)),
                  pl.BlockSpec(memory_space=pl.ANY)],
        out_specs=pl.BlockSpec((L, LANES), lambda b, j: (b + blk0, out0 // LANES + j)),
        out_shape=jax.ShapeDtypeStruct((T, D), BF16),
        scratch_shapes=[pltpu.VMEM((L + 2 * MARGIN, LANES), F32)]
        + [pltpu.VMEM((2, L, LANES), F32)] * 4
        + [pltpu.VMEM((2, MIX_CHUNK, 2 * LANES), F32)] * 2
        + [pltpu.VMEM((2, 3 * SCAN_SEGMENTS, LANES), F32)],
        input_output_aliases={7: 0},
        name="mix_lru",
        **_mixer_params(nseq, nheads, 15 * L * LANES * 4),
    )(proj, proj, cw, cb, wg, bg, lam, y)


def _pool_call(proj, y, pool_w, pool_scale, layer, L, nseq, blk0, col0, out0):
    T, D = y.shape
    width = POOL_PACK * LANES
    return pl.pallas_call(
        functools.partial(_pool_kernel, L=L),
        in_specs=[pl.BlockSpec((L, width), lambda b, j: (b + blk0, col0 // width + j)),
                  pl.BlockSpec((None, None, width, width), lambda b, j: (layer, j, 0, 0)),
                  pl.BlockSpec((None, 1, width), lambda b, j: (layer, 0, j)),
                  pl.BlockSpec(memory_space=pl.ANY)],
        out_specs=pl.BlockSpec((L, width), lambda b, j: (b + blk0, out0 // width + j)),
        out_shape=jax.ShapeDtypeStruct((T, D), BF16),
        scratch_shapes=[pltpu.VMEM((L + 2 * MARGIN, LANES), F32), pltpu.VMEM((L, width), BF16)],
        input_output_aliases={3: 0},
        name="mix_pool",
        **_mixer_params(nseq, len(POOL_WINDOWS) // POOL_PACK, 5 * L * width * 4),
    )(proj, pool_w, pool_scale, y)


def kernel(x_prompt, x_sample, meta_tokens, ln_in_g, ln_in_b, ffn1_w_gate, ffn1_w_up, ffn1_w_down, ln1_g, ln1_b, w_in, conv_w, conv_b, lru_conv_w, lru_conv_b, lru_w_a, lru_b_a, lru_w_x, lru_b_x, lru_lambda, pool_w, pool_scale, w_out, ln2_g, ln2_b, ffn2_w_gate, ffn2_w_up, ffn2_w_down, ln3_g, ln3_b):
    depth = ffn1_w_gate.shape[0]
    alpha = float((2 * depth) ** 0.25)
    bp, sp, D = x_prompt.shape
    bs, ss, _ = x_sample.shape
    lp, ls = sp + N_META, ss + N_META
    dconv = conv_w.shape[-1]
    dlru = lru_lambda.shape[-1]
    dpool = pool_scale.shape[-1]
    nheads = lru_w_a.shape[2]
    assert D == dconv + dlru + dpool and dlru // nheads == LANES and dpool // len(POOL_WINDOWS) == LANES
    assert dconv % LANES == 0 and sp % SEQ_TILE == 0 and ss % SEQ_TILE == 0 and ROW_TILE % PROJ_ROWS == 0
    assert lp % SCAN_SEGMENTS == 0 and ls % SCAN_SEGMENTS == 0
    assert max(POOL_WINDOWS) // 2 <= min(N_META, MARGIN) and POOL_PACK == 2

    ts = bs * ls
    blk0_p = -(-ts // lp)
    off_p = blk0_p * lp
    t_real = off_p + bp * lp
    t_pad = -(-t_real // ROW_TILE) * ROW_TILE
    groups = ((x_sample, ls, bs, 0, ss), (x_prompt, lp, bp, blk0_p, sp))

    bf = lambda w: w.astype(BF16)
    row3 = lambda p: p.reshape(depth, 1, -1)
    f1 = (bf(ffn1_w_gate), bf(ffn1_w_up), bf(ffn1_w_down), row3(ln1_g), row3(ln1_b))
    f2 = (bf(ffn2_w_gate), bf(ffn2_w_up), bf(ffn2_w_down), row3(ln3_g), row3(ln3_b))
    w_in_b, w_out_b = bf(w_in), bf(w_out)
    gate_w = bf(0.5 * jnp.concatenate([lru_w_a, lru_w_x], axis=-1)).transpose(0, 2, 1, 3, 4)
    gate_b = 0.5 * jnp.concatenate([lru_b_a.reshape(depth, 2, nheads, LANES),
                                    lru_b_x.reshape(depth, 2, nheads, LANES)], axis=-1).transpose(0, 2, 1, 3)
    pw = bf(pool_w).reshape(depth, -1, POOL_PACK, LANES, LANES)
    zero = jnp.zeros_like(pw[:, :, 0])
    pool_w_b = jnp.concatenate([jnp.concatenate([pw[:, :, 0], zero], axis=-1),
                                jnp.concatenate([zero, pw[:, :, 1]], axis=-1)], axis=-2)
    conv_b3, pool_s3 = row3(conv_b), row3(pool_scale)
    ln2 = (row3(ln2_g), row3(ln2_b))
    lru_col, pool_col = 3 * dconv, 3 * dconv + 2 * dlru

    h = jnp.zeros((t_pad, D), F32)
    y = jnp.zeros((t_pad, D), BF16)
    for x, L, nseq, blk0, S in groups:
        h = _ln_in_call(x, h, ln_in_g, ln_in_b, L, blk0, SEQ_TILE, 0)
        meta = jnp.broadcast_to(meta_tokens[None], (nseq, N_META, D))
        h = _ln_in_call(meta, h, ln_in_g, ln_in_b, L, blk0, N_META, S)

    outs = []
    for layer in range(depth):
        h = _ffn_call(h, *f1, layer, alpha)
        proj = _inproj_call(h, w_in_b, layer)
        for _, L, nseq, blk0, _ in groups:
            y = _conv_call(proj, y, conv_w, conv_b3, layer, L, nseq, blk0, dconv)
            y = _lru_call(proj, y, lru_conv_w, lru_conv_b, gate_w, gate_b, lru_lambda,
                          layer, L, nseq, blk0, lru_col, dconv, nheads)
            y = _pool_call(proj, y, pool_w_b, pool_s3, layer, L, nseq, blk0, pool_col, dconv + dlru)
        h = _outproj_call(h, y, w_out_b, *ln2, layer, alpha)
        if layer + 1 < depth:
            h = _ffn_call(h, *f2, layer, alpha)
        else:
            outs = [_ffn_call(h, *f2, layer, alpha, seq=(L, nseq, blk0, S)).reshape(nseq, S, D)
                    for _, L, nseq, blk0, S in groups]

    y_sample, y_prompt = outs
    return (y_prompt, y_sample)
```

```python
import functools

import jax
import jax.numpy as jnp
from jax import lax
from jax.experimental import pallas as pl
from jax.experimental.pallas import tpu as pltpu

F32 = jnp.float32
BF16 = jnp.bfloat16

N_META = 16
POOL_WINDOWS = (2, 4, 8, 16)
SHORT_CONV_WIDTH = 3
LRU_CONV_WIDTH = 4
LRU_C = 8.0
LN_EPS = 1e-5
LOG2_E = 1.4426950408889634
GELU_C0 = 0.7978845608028654
GELU_C1 = 0.044715

LANES = 128
SUBLANES = 8
VMEM_BYTES_V7X = 64 * 1024 * 1024

ROW_TILE = 512
PROJ_ROWS = 512
SEQ_TILE = 512
FF_TILE = 512
PROJ_TILE = 1024
MIX_CHUNK = 1024
POOL_PACK = 2
CONV_PACK = 2
COPY_CHUNK = 256
MARGIN = SUBLANES
SCAN_SEGMENTS = 2 * SUBLANES


def _vmem_limit(block_bytes):
    return int(min(block_bytes + block_bytes // 2 + (8 << 20), VMEM_BYTES_V7X - (6 << 20)))


def _layernorm(z, g, b):
    mu = jnp.mean(z, axis=-1, keepdims=True)
    d = z - mu
    var = jnp.mean(d * d, axis=-1, keepdims=True)
    return d * lax.rsqrt(var + LN_EPS) * g + b


def _ln_kernel(x_ref, g_ref, b_ref, hprev_ref, o_ref):
    del hprev_ref
    o_ref[...] = _layernorm(x_ref[...], g_ref[...], b_ref[...])


def _ln_in_call(x, h, g, b, L, blk0, rows_per_step, row0):
    nseq, S, D = x.shape
    T = h.shape[0]
    nj = S // rows_per_step
    return pl.pallas_call(
        _ln_kernel,
        grid=(nseq, nj),
        in_specs=[pl.BlockSpec((None, rows_per_step, D), lambda s, j: (s, j, 0)),
                  pl.BlockSpec((1, D), lambda s, j: (0, 0)),
                  pl.BlockSpec((1, D), lambda s, j: (0, 0)),
                  pl.BlockSpec(memory_space=pl.ANY)],
        out_specs=pl.BlockSpec((pl.Element(rows_per_step), pl.Element(D)),
                               lambda s, j: (pl.multiple_of((blk0 + s) * L + row0 + j * rows_per_step,
                                                            SUBLANES), 0)),
        out_shape=jax.ShapeDtypeStruct((T, D), F32),
        input_output_aliases={3: 0},
        compiler_params=pltpu.CompilerParams(
            dimension_semantics=("arbitrary", "arbitrary"),
            vmem_limit_bytes=_vmem_limit(4 * rows_per_step * D * 4)),
        name="ln_in",
    )(x, g.reshape(1, D), b.reshape(1, D), h)


def _ffn_kernel(x_ref, wg_hbm, wu_hbm, wd_hbm, g_ref, b_ref, o_ref,
                wg_buf, wu_buf, wd_buf, sem, xb_ref, acc_ref, z_ref, *, nk, n_tiles, layer, alpha):
    i = pl.program_id(0)
    real = i < n_tiles
    ln_steps = 1 << ((nk - 2).bit_length() - 1)
    row_tile = x_ref.shape[0]
    ln_rows = row_tile // ln_steps

    def chunk_copies(c, slot):
        cols = pl.ds(c * FF_TILE, FF_TILE)
        return (pltpu.make_async_copy(wg_hbm.at[layer, :, cols], wg_buf.at[slot], sem.at[0, slot]),
                pltpu.make_async_copy(wu_hbm.at[layer, :, cols], wu_buf.at[slot], sem.at[1, slot]),
                pltpu.make_async_copy(wd_hbm.at[layer, cols, :], wd_buf.at[slot], sem.at[2, slot]))

    def normalise(rows):
        o_ref[rows] = _layernorm(z_ref[rows], g_ref[...], b_ref[...])

    @pl.when(i == 0)
    def _():
        z_ref[...] = jnp.zeros_like(z_ref)
        for copy in chunk_copies(0, 0):
            copy.start()

    @pl.when(real)
    def _():
        first = (i & 1) if nk % 2 else 0
        xb_ref[...] = x_ref[...].astype(BF16)
        for c in range(nk):
            slot = first ^ (c & 1)
            if c + 1 < nk:
                for copy in chunk_copies(c + 1, 1 - slot):
                    copy.start()
            else:
                @pl.when(i + 1 < n_tiles)
                def _():
                    for copy in chunk_copies(0, 1 - slot):
                        copy.start()
            for copy in chunk_copies(c, slot):
                copy.wait()
            xb = xb_ref[...]
            gate = jnp.dot(xb, wg_buf[slot], preferred_element_type=F32)
            up = jnp.dot(xb, wu_buf[slot], preferred_element_type=F32)
            hidden = (gate * jax.nn.sigmoid(gate) * up).astype(BF16)
            part = jnp.dot(hidden, wd_buf[slot], preferred_element_type=F32)
            if 1 <= c <= ln_steps:
                normalise(pl.ds((c - 1) * ln_rows, ln_rows))
            if c == 0:
                acc_ref[...] = part
            elif c + 1 < nk:
                acc_ref[...] += part
            else:
                z_ref[...] = alpha * x_ref[...] + 0.5 * (acc_ref[...] + part)

    @pl.when(jnp.logical_not(real))
    def _():
        normalise(pl.ds(0, row_tile))


def _ffn_call(h, wg, wu, wd, g, b, layer, alpha, seq=None):
    T, D = h.shape
    F = wg.shape[-1]
    nk = F // FF_TILE
    assert nk >= 3
    if seq is None:
        rt = ROW_TILE
        n_tiles, out_rows = T // rt, T
        x_spec = pl.BlockSpec((rt, D), lambda i: (jnp.minimum(i, n_tiles - 1), 0))
    else:
        L, nseq, blk0, S = seq
        rt = SEQ_TILE
        nj = S // rt
        n_tiles, out_rows = nseq * nj, nseq * S

        def x_rows(i):
            t = jnp.minimum(i, n_tiles - 1)
            return (pl.multiple_of((blk0 + t // nj) * L + (t % nj) * rt, SUBLANES), 0)

        x_spec = pl.BlockSpec((pl.Element(rt), pl.Element(D)), x_rows)
    blocks = 4 * rt * D * 4 + 6 * D * FF_TILE * 2 + rt * D * (2 + 4 + 4) + 3 * rt * FF_TILE * 4
    return pl.pallas_call(
        functools.partial(_ffn_kernel, nk=nk, n_tiles=n_tiles, layer=layer, alpha=alpha),
        grid=(n_tiles + 1,),
        in_specs=[x_spec,
                  pl.BlockSpec(memory_space=pl.ANY),
                  pl.BlockSpec(memory_space=pl.ANY),
                  pl.BlockSpec(memory_space=pl.ANY),
                  pl.BlockSpec((None, 1, D), lambda i: (layer, 0, 0)),
                  pl.BlockSpec((None, 1, D), lambda i: (layer, 0, 0))],
        out_specs=pl.BlockSpec((rt, D), lambda i: (jnp.maximum(i - 1, 0), 0)),
        out_shape=jax.ShapeDtypeStruct((out_rows, D), F32),
        scratch_shapes=[pltpu.VMEM((2, D, FF_TILE), BF16), pltpu.VMEM((2, D, FF_TILE), BF16),
                        pltpu.VMEM((2, FF_TILE, D), BF16), pltpu.SemaphoreType.DMA((3, 2)),
                        pltpu.VMEM((rt, D), BF16), pltpu.VMEM((rt, D), F32),
                        pltpu.VMEM((rt, D), F32)],
        compiler_params=pltpu.CompilerParams(
            dimension_semantics=("arbitrary",),
            vmem_limit_bytes=_vmem_limit(blocks)),
        name="ffn",
    )(h, wg, wu, wd, g, b)


def _inproj_kernel(x_ref, w_ref, o_ref):
    xb = x_ref[...].astype(BF16)
    for n in range(o_ref.shape[-1] // PROJ_TILE):
        cols = slice(n * PROJ_TILE, (n + 1) * PROJ_TILE)
        o_ref[:, cols] = jnp.dot(xb, w_ref[:, cols], preferred_element_type=F32)


def _inproj_call(h, w_in, layer):
    T, D = h.shape
    P = w_in.shape[-1]
    blocks = 2 * PROJ_ROWS * D * 4 + D * P * 2 + 2 * PROJ_ROWS * P * 4 + PROJ_ROWS * D * 2
    return pl.pallas_call(
        _inproj_kernel,
        grid=(T // PROJ_ROWS,),
        in_specs=[pl.BlockSpec((PROJ_ROWS, D), lambda i: (i, 0)),
                  pl.BlockSpec((None, D, P), lambda i: (layer, 0, 0), pipeline_mode=pl.Buffered(1))],
        out_specs=pl.BlockSpec((PROJ_ROWS, P), lambda i: (i, 0)),
        out_shape=jax.ShapeDtypeStruct((T, P), F32),
        compiler_params=pltpu.CompilerParams(
            dimension_semantics=("parallel",),
            vmem_limit_bytes=_vmem_limit(blocks)),
        name="in_proj",
    )(h, w_in)


def _outproj_kernel(h_ref, y_ref, w_ref, g_ref, b_ref, o_ref, *, alpha):
    mix = jnp.dot(y_ref[...], w_ref[...], preferred_element_type=F32)
    o_ref[...] = _layernorm(alpha * h_ref[...] + mix, g_ref[...], b_ref[...])


def _outproj_call(h, y, w_out, g, b, layer, alpha):
    T, D = h.shape
    rt = PROJ_ROWS
    blocks = 4 * rt * D * 4 + 2 * rt * D * 2 + D * D * 2 + 2 * rt * D * 4
    return pl.pallas_call(
        functools.partial(_outproj_kernel, alpha=alpha),
        grid=(T // rt,),
        in_specs=[pl.BlockSpec((rt, D), lambda i: (i, 0)),
                  pl.BlockSpec((rt, D), lambda i: (i, 0)),
                  pl.BlockSpec((None, D, D), lambda i: (layer, 0, 0), pipeline_mode=pl.Buffered(1)),
                  pl.BlockSpec((None, 1, D), lambda i: (layer, 0, 0)),
                  pl.BlockSpec((None, 1, D), lambda i: (layer, 0, 0))],
        out_specs=pl.BlockSpec((rt, D), lambda i: (i, 0)),
        out_shape=jax.ShapeDtypeStruct((T, D), F32),
        compiler_params=pltpu.CompilerParams(
            dimension_semantics=("parallel",),
            vmem_limit_bytes=_vmem_limit(blocks)),
        name="out_proj",
    )(h, y, w_out, g, b)


def _fill_padded(dst_ref, L, read):
    lx = L - N_META
    zeros = jnp.zeros((MARGIN, LANES), F32)
    dst_ref[0:MARGIN] = zeros
    dst_ref[MARGIN + L:2 * MARGIN + L] = zeros
    dst_ref[MARGIN:MARGIN + N_META] = read(pl.ds(lx, N_META))

    def body(i, c):
        r = pl.multiple_of(i * COPY_CHUNK, COPY_CHUNK)
        dst = pl.multiple_of(r + (MARGIN + N_META), SUBLANES)
        dst_ref[pl.ds(dst, COPY_CHUNK)] = read(pl.ds(r, COPY_CHUNK))
        return c

    lax.fori_loop(0, lx // COPY_CHUNK, body, 0)


def _for_storage_chunks(L, fn):
    lx = L - N_META
    last = lx // COPY_CHUNK - 1

    def body(i, c):
        r = pl.multiple_of(i * COPY_CHUNK, COPY_CHUNK)
        fn(pl.ds(r, COPY_CHUNK), r + N_META, COPY_CHUNK, False)
        return c

    lax.fori_loop(0, last, body, 0)
    fn(pl.ds(last * COPY_CHUNK, COPY_CHUNK), last * COPY_CHUNK + N_META, COPY_CHUNK, True)
    fn(pl.ds(lx, N_META), 0, N_META, True)


def _for_logical_chunks(L, chunk, fn):
    n_main = L // chunk

    def body(i, c):
        fn(pl.multiple_of(i * chunk, chunk), chunk)
        return c

    lax.fori_loop(0, n_main, body, 0)
    if L % chunk:
        fn(n_main * chunk, L % chunk)


def _conv_kernel(bg_ref, cg_ref, v_ref, w_ref, cb_ref, yprev_ref, o_ref, upad_ref, *, L):
    del yprev_ref
    w = w_ref[...]
    cb = cb_ref[...]

    for part in range(CONV_PACK):
        lanes = slice(part * LANES, (part + 1) * LANES)
        _fill_padded(upad_ref, L, lambda rows, lanes=lanes: cg_ref[rows, lanes] * v_ref[rows, lanes])

        def emit(rows, t0, n, at_end, lanes=lanes):
            del at_end
            acc = cb[:, lanes]
            for k in range(SHORT_CONV_WIDTH):
                off = k - SHORT_CONV_WIDTH // 2
                acc = acc + w[k:k + 1, lanes] * upad_ref[pl.ds(t0 + (MARGIN + off), n)]
            o_ref[rows, lanes] = (bg_ref[rows, lanes] * acc).astype(BF16)

        _for_storage_chunks(L, emit)


def _pool_kernel(u_ref, pw_ref, ps_ref, yprev_ref, o_ref, upad_ref, pooled_ref, *, L):
    del yprev_ref
    pack = pl.program_id(1)

    for part in range(POOL_PACK):
        lanes = slice(part * LANES, (part + 1) * LANES)
        _fill_padded(upad_ref, L, lambda rows, lanes=lanes: u_ref[rows, lanes])

        for gi, win in enumerate(POOL_WINDOWS):
            if gi % POOL_PACK != part:
                continue
            half = win // 2

            @pl.when(pack == gi // POOL_PACK)
            def _(win=win, half=half, lanes=lanes):
                def emit(rows, t0, n, at_end):
                    total = upad_ref[pl.ds(t0 + (MARGIN - half), n)]
                    for j in range(1 - half, half):
                        total = total + upad_ref[pl.ds(t0 + (MARGIN + j), n)]
                    if at_end:
                        t = t0 + lax.broadcasted_iota(jnp.int32, (n, LANES), 0)
                        lo = jnp.maximum(t - half, 0)
                        hi = jnp.minimum(t + (half - 1), L - 1)
                        mean = total / (hi - lo + 1).astype(F32)
                    else:
                        mean = total * (1.0 / win)
                    pooled_ref[rows, lanes] = (mean - upad_ref[pl.ds(t0 + MARGIN, n)]).astype(BF16)

                _for_storage_chunks(L, emit)

    mixed = jnp.dot(pooled_ref[...], pw_ref[...], preferred_element_type=F32)
    o_ref[...] = (mixed * ps_ref[...]).astype(BF16)


def _lru_kernel(x_ref, gate_ref, cw_ref, cb_ref, wg_ref, bg_ref, lam_ref, yprev_ref, o_ref,
                xpad_ref, xc_ref, a_ref, b_ref, h_ref, pre0_ref, pre1_ref, carry_ref, *, L):
    del yprev_ref
    seg = L // SCAN_SEGMENTS
    _fill_padded(xpad_ref, L, lambda rows: x_ref[rows])

    cw = cw_ref[...]
    cb = cb_ref[...]
    bg = bg_ref[...]
    neg_lam = -lam_ref[...]
    softplus = jnp.maximum(neg_lam, 0.0) + jnp.log1p(jnp.exp(-jnp.abs(neg_lam)))
    half_c2 = (-0.5 * LRU_C * LOG2_E) * softplus

    def conv(t0, n):
        for d in range(2):
            xc = cb[d:d + 1]
            for k in range(LRU_CONV_WIDTH):
                off = (k - (LRU_CONV_WIDTH - 1)) if d == 0 else ((LRU_CONV_WIDTH - 1) - k)
                xc = xc + cw[d, k:k + 1] * xpad_ref[pl.ds(t0 + (MARGIN + off), n)]
            xc_ref[d, pl.ds(t0, n)] = xc

    _for_logical_chunks(L, MIX_CHUNK, conv)

    def gate_matmuls(t0, n, pre_ref):
        for d in range(2):
            pre_ref[d, 0:n] = jnp.dot(xc_ref[d, pl.ds(t0, n)].astype(BF16), wg_ref[d],
                                      preferred_element_type=F32)

    def finish(t0, n, pre_ref):
        for d in range(2):
            pre = pre_ref[d, 0:n] + bg[d:d + 1]
            t_r = jnp.tanh(pre[:, :LANES])
            t_i = jnp.tanh(pre[:, LANES:])
            hc = half_c2[d:d + 1]
            a = jnp.exp2(hc + hc * t_r)
            gated = (0.5 * xc_ref[d, pl.ds(t0, n)]) * (1.0 + t_i)
            a_ref[d, pl.ds(t0, n)] = a
            v = 1.0 - a * a
            b_ref[d, pl.ds(t0, n)] = jnp.where(v > 0.0, v * lax.rsqrt(v), 0.0) * gated

    n_main = L // MIX_CHUNK
    assert n_main % 2 == 0
    gate_matmuls(0, MIX_CHUNK, pre0_ref)

    def gates_body(i, c):
        c0 = pl.multiple_of(2 * i * MIX_CHUNK, MIX_CHUNK)
        c1 = pl.multiple_of(c0 + MIX_CHUNK, MIX_CHUNK)
        nxt = pl.multiple_of(jnp.minimum(2 * i + 2, n_main - 1) * MIX_CHUNK, MIX_CHUNK)
        gate_matmuls(c1, MIX_CHUNK, pre1_ref)
        finish(c0, MIX_CHUNK, pre0_ref)
        gate_matmuls(nxt, MIX_CHUNK, pre0_ref)
        finish(c1, MIX_CHUNK, pre1_ref)
        return c

    lax.fori_loop(0, n_main // 2, gates_body, 0)
    if L % MIX_CHUNK:
        t0, n = n_main * MIX_CHUNK, L % MIX_CHUNK
        gate_matmuls(t0, n, pre0_ref)
        finish(t0, n, pre0_ref)

    nv = SCAN_SEGMENTS // SUBLANES

    def seg_rows(j, t):
        return pl.ds(j * SUBLANES * seg + t, SUBLANES, stride=seg)

    def ends_body(t, c):
        out = []
        for d in range(2):
            tt = t if d == 0 else seg - 1 - t
            for j in range(nv):
                h, p = c[2 * (d * nv + j)], c[2 * (d * nv + j) + 1]
                rows = seg_rows(j, tt)
                a = a_ref[d, rows]
                out += [a * h + b_ref[d, rows], p * a]
        return tuple(out)

    zero = jnp.zeros((SUBLANES, LANES), F32)
    one = jnp.ones((SUBLANES, LANES), F32)
    ends = lax.fori_loop(0, seg, ends_body, (zero, one) * (2 * nv), unroll=4)

    S = SCAN_SEGMENTS
    for d in range(2):
        for j in range(nv):
            carry_ref[d, pl.ds(j * SUBLANES, SUBLANES)] = ends[2 * (d * nv + j)]
            carry_ref[d, pl.ds(S + j * SUBLANES, SUBLANES)] = ends[2 * (d * nv + j) + 1]
        state = jnp.zeros((1, LANES), F32)
        for s in (range(S) if d == 0 else reversed(range(S))):
            carry_ref[d, 2 * S + s:2 * S + s + 1] = state
            state = carry_ref[d, S + s:S + s + 1] * state + carry_ref[d, s:s + 1]
    cin = tuple(carry_ref[d, pl.ds(2 * S + j * SUBLANES, SUBLANES)] for d in range(2) for j in range(nv))

    def scan_body(t, c):
        out = []
        for d in range(2):
            tt = t if d == 0 else seg - 1 - t
            for j in range(nv):
                rows = seg_rows(j, tt)
                h = a_ref[d, rows] * c[d * nv + j] + b_ref[d, rows]
                h_ref[d, rows] = h
                out.append(h)
        return tuple(out)

    lax.fori_loop(0, seg, scan_body, cin, unroll=4)

    def emit(rows, t0, n, at_end):
        del at_end
        g = gate_ref[rows]
        t = jnp.tanh(g * (GELU_C0 + (GELU_C0 * GELU_C1) * (g * g)))
        hsum = h_ref[0, pl.ds(t0, n)] + h_ref[1, pl.ds(t0, n)]
        o_ref[rows] = (((0.5 * g) * (1.0 + t)) * hsum).astype(BF16)

    _for_storage_chunks(L, emit)


def _mixer_params(nseq, nblk, scratch_bytes):
    return dict(
        grid=(nseq, nblk),
        compiler_params=pltpu.CompilerParams(
            dimension_semantics=("parallel", "arbitrary"),
            vmem_limit_bytes=_vmem_limit(scratch_bytes)),
    )


def _conv_call(proj, y, conv_w, conv_b, layer, L, nseq, blk0, dconv):
    width = CONV_PACK * LANES
    nb = dconv // width
    T, D = y.shape
    seq = lambda base: pl.BlockSpec((L, width), lambda b, j: (b + blk0, base + j))
    return pl.pallas_call(
        functools.partial(_conv_kernel, L=L),
        in_specs=[seq(0), seq(nb), seq(2 * nb),
                  pl.BlockSpec((None, SHORT_CONV_WIDTH, width), lambda b, j: (layer, 0, j)),
                  pl.BlockSpec((None, 1, width), lambda b, j: (layer, 0, j)),
                  pl.BlockSpec(memory_space=pl.ANY)],
        out_specs=pl.BlockSpec((L, width), lambda b, j: (b + blk0, j)),
        out_shape=jax.ShapeDtypeStruct((T, D), BF16),
        scratch_shapes=[pltpu.VMEM((L + 2 * MARGIN, LANES), F32)],
        input_output_aliases={5: 0},
        name="mix_conv",
        **_mixer_params(nseq, nb, 9 * L * width * 4),
    )(proj, proj, proj, conv_w, conv_b, y)


def _lru_call(proj, y, cw, cb, wg, bg, lam, layer, L, nseq, blk0, col0, out0, nheads):
    T, D = y.shape
    c0 = col0 // LANES
    return pl.pallas_call(
        functools.partial(_lru_kernel, L=L),
        in_specs=[pl.BlockSpec((L, LANES), lambda b, j: (b + blk0, c0 + j)),
                  pl.BlockSpec((L, LANES), lambda b, j: (b + blk0, c0 + nheads + j)),
                  pl.BlockSpec((None, 2, LRU_CONV_WIDTH, LANES), lambda b, j: (layer, 0, 0, j)),
                  pl.BlockSpec((None, 2, LANES), lambda b, j: (layer, 0, j)),
                  pl.BlockSpec((None, None, 2, LANES, 2 * LANES), lambda b, j: (layer, j, 0, 0, 0)),
                  pl.BlockSpec((None, None, 2, 2 * LANES), lambda b, j: (layer, j, 0, 0)),
                  pl.BlockSpec((None, 2, LANES), lambda b, j: (layer, 0, j)),
                  pl.BlockSpec(memory_space=pl.ANY)],
        out_specs=pl.BlockSpec((L, LANES), lambda b, j: (b + blk0, out0 // LANES + j)),
        out_shape=jax.ShapeDtypeStruct((T, D), BF16),
        scratch_shapes=[pltpu.VMEM((L + 2 * MARGIN, LANES), F32)]
        + [pltpu.VMEM((2, L, LANES), F32)] * 4
        + [pltpu.VMEM((2, MIX_CHUNK, 2 * LANES), F32)] * 2
        + [pltpu.VMEM((2, 3 * SCAN_SEGMENTS, LANES), F32)],
        input_output_aliases={7: 0},
        name="mix_lru",
        **_mixer_params(nseq, nheads, 15 * L * LANES * 4),
    )(proj, proj, cw, cb, wg, bg, lam, y)


def _pool_call(proj, y, pool_w, pool_scale, layer, L, nseq, blk0, col0, out0):
    T, D = y.shape
    width = POOL_PACK * LANES
    return pl.pallas_call(
        functools.partial(_pool_kernel, L=L),
        in_specs=[pl.BlockSpec((L, width), lambda b, j: (b + blk0, col0 // width + j)),
                  pl.BlockSpec((None, None, width, width), lambda b, j: (layer, j, 0, 0)),
                  pl.BlockSpec((None, 1, width), lambda b, j: (layer, 0, j)),
                  pl.BlockSpec(memory_space=pl.ANY)],
        out_specs=pl.BlockSpec((L, width), lambda b, j: (b + blk0, out0 // width + j)),
        out_shape=jax.ShapeDtypeStruct((T, D), BF16),
        scratch_shapes=[pltpu.VMEM((L + 2 * MARGIN, LANES), F32), pltpu.VMEM((L, width), BF16)],
        input_output_aliases={3: 0},
        name="mix_pool",
        **_mixer_params(nseq, len(POOL_WINDOWS) // POOL_PACK, 5 * L * width * 4),
    )(proj, pool_w, pool_scale, y)


def kernel(x_prompt, x_sample, meta_tokens, ln_in_g, ln_in_b, ffn1_w_gate, ffn1_w_up, ffn1_w_down, ln1_g, ln1_b, w_in, conv_w, conv_b, lru_conv_w, lru_conv_b, lru_w_a, lru_b_a, lru_w_x, lru_b_x, lru_lambda, pool_w, pool_scale, w_out, ln2_g, ln2_b, ffn2_w_gate, ffn2_w_up, ffn2_w_down, ln3_g, ln3_b):
    depth = ffn1_w_gate.shape[0]
    alpha = float((2 * depth) ** 0.25)
    bp, sp, D = x_prompt.shape
    bs, ss, _ = x_sample.shape
    lp, ls = sp + N_META, ss + N_META
    dconv = conv_w.shape[-1]
    dlru = lru_lambda.shape[-1]
    dpool = pool_scale.shape[-1]
    nheads = lru_w_a.shape[2]
    assert D == dconv + dlru + dpool and dlru // nheads == LANES and dpool // len(POOL_WINDOWS) == LANES
    assert dconv % (CONV_PACK * LANES) == 0 and sp % SEQ_TILE == 0 and ss % SEQ_TILE == 0 and ROW_TILE % PROJ_ROWS == 0
    assert lp % SCAN_SEGMENTS == 0 and ls % SCAN_SEGMENTS == 0
    assert max(POOL_WINDOWS) // 2 <= min(N_META, MARGIN) and POOL_PACK == 2

    ts = bs * ls
    blk0_p = -(-ts // lp)
    off_p = blk0_p * lp
    t_real = off_p + bp * lp
    t_pad = -(-t_real // ROW_TILE) * ROW_TILE
    groups = ((x_sample, ls, bs, 0, ss), (x_prompt, lp, bp, blk0_p, sp))

    bf = lambda w: w.astype(BF16)
    row3 = lambda p: p.reshape(depth, 1, -1)
    f1 = (bf(ffn1_w_gate), bf(ffn1_w_up), bf(ffn1_w_down), row3(ln1_g), row3(ln1_b))
    f2 = (bf(ffn2_w_gate), bf(ffn2_w_up), bf(ffn2_w_down), row3(ln3_g), row3(ln3_b))
    w_in_b, w_out_b = bf(w_in), bf(w_out)
    gate_w = bf(0.5 * jnp.concatenate([lru_w_a, lru_w_x], axis=-1)).transpose(0, 2, 1, 3, 4)
    gate_b = 0.5 * jnp.concatenate([lru_b_a.reshape(depth, 2, nheads, LANES),
                                    lru_b_x.reshape(depth, 2, nheads, LANES)], axis=-1).transpose(0, 2, 1, 3)
    pw = bf(pool_w).reshape(depth, -1, POOL_PACK, LANES, LANES)
    zero = jnp.zeros_like(pw[:, :, 0])
    pool_w_b = jnp.concatenate([jnp.concatenate([pw[:, :, 0], zero], axis=-1),
                                jnp.concatenate([zero, pw[:, :, 1]], axis=-1)], axis=-2)
    conv_b3, pool_s3 = row3(conv_b), row3(pool_scale)
    ln2 = (row3(ln2_g), row3(ln2_b))
    lru_col, pool_col = 3 * dconv, 3 * dconv + 2 * dlru

    h = jnp.zeros((t_pad, D), F32)
    y = jnp.zeros((t_pad, D), BF16)
    for x, L, nseq, blk0, S in groups:
        h = _ln_in_call(x, h, ln_in_g, ln_in_b, L, blk0, SEQ_TILE, 0)
        meta = jnp.broadcast_to(meta_tokens[None], (nseq, N_META, D))
        h = _ln_in_call(meta, h, ln_in_g, ln_in_b, L, blk0, N_META, S)

    outs = []
    for layer in range(depth):
        h = _ffn_call(h, *f1, layer, alpha)
        proj = _inproj_call(h, w_in_b, layer)
        for _, L, nseq, blk0, _ in groups:
            y = _conv_call(proj, y, conv_w, conv_b3, layer, L, nseq, blk0, dconv)
            y = _lru_call(proj, y, lru_conv_w, lru_conv_b, gate_w, gate_b, lru_lambda,
                          layer, L, nseq, blk0, lru_col, dconv, nheads)
            y = _pool_call(proj, y, pool_w_b, pool_s3, layer, L, nseq, blk0, pool_col, dconv + dlru)
        h = _outproj_call(h, y, w_out_b, *ln2, layer, alpha)
        if layer + 1 < depth:
            h = _ffn_call(h, *f2, layer, alpha)
        else:
            outs = [_ffn_call(h, *f2, layer, alpha, seq=(L, nseq, blk0, S)).reshape(nseq, S, D)
                    for _, L, nseq, blk0, S in groups]

    y_sample, y_prompt = outs
    return (y_prompt, y_sample)
```
